```python
import math
import jax, jax.numpy as jnp
from jax import lax
import numpy as np

D_MODEL = 1024
BATCH = 8
SEQ = 2048
DEPTH = 4
DEC_BATCH = 128
DEC_SEQ = 8
PAST_LEN = 16384
PAGE_SIZE = 128

GM_WIDTH = 1024
GM_GROUPS = 4
GM_GROUP_DIM = GM_WIDTH // GM_GROUPS
GM_CHUNK = 128
HG_HEADS = 8
HG_DK = 128
HG_DV = 128
HG_WIDTH = HG_HEADS * HG_DK
HG_CHUNK = 64
D_FF = 2816
N_EXPERTS = 8
TOP_K = 2
D_FF_EXPERT = 2816
N_DENSE = (DEPTH + 1) // 2
N_MOE = DEPTH // 2
IN_COLS = 2 * GM_WIDTH + 4 * HG_WIDTH + 2 * D_MODEL
EPS = 1e-6

kernel_name = "hybrid_gmlp_hgrn2_decoder_step"


def _rmsnorm(x, w):
    xf = x.astype(jnp.float32)
    y = xf * lax.rsqrt(jnp.mean(xf * xf, axis=-1, keepdims=True) + EPS)
    return (y * w.astype(jnp.float32)).astype(x.dtype)


def _layernorm(x, w, b):
    xf = x.astype(jnp.float32)
    mu = jnp.mean(xf, axis=-1, keepdims=True)
    xc = xf - mu
    y = xc * lax.rsqrt(jnp.mean(xc * xc, axis=-1, keepdims=True) + EPS)
    return (y * w.astype(jnp.float32) + b.astype(jnp.float32)).astype(x.dtype)


def _lower_bounds(raw):
    p = jax.nn.softmax(raw.astype(jnp.float32), axis=0)
    c = jnp.cumsum(p, axis=0)
    return c - c[0:1]


def _spatial_gate(v, w_s, b_s):
    B, T, _ = v.shape
    L = min(T, GM_CHUNK)
    n = T // L
    mask = jnp.tril(jnp.ones((L, L), dtype=bool))
    w = jnp.where(mask[None], w_s[:, :L, :L], 0.0).astype(v.dtype)
    vc = v.reshape(B, n, L, GM_GROUPS, GM_GROUP_DIM)
    mixed = jnp.einsum('gts,bnsgc->bntgc', w, vc)
    mixed = mixed + b_s[:, :L].T.astype(v.dtype)[None, None, :, :, None]
    return mixed.reshape(B, T, GM_WIDTH)


def _hgrn2(q, logf, i, s0):
    B, T = q.shape[:2]
    C = math.gcd(T, HG_CHUNK)
    n = T // C
    k = -jnp.expm1(logf)

    def to_chunks(a):
        return a.astype(jnp.float32).reshape(B, n, C, HG_HEADS, a.shape[-1]).transpose(1, 0, 3, 2, 4)

    qc, gc, kc, ic = to_chunks(q), to_chunks(logf), to_chunks(k), to_chunks(i)
    causal = jnp.tril(jnp.ones((C, C), dtype=bool))[:, :, None]

    def step(S, xs):
        qb, gb, kb, ib = xs
        bcum = jnp.cumsum(gb, axis=2)
        diff = bcum[:, :, :, None, :] - bcum[:, :, None, :, :]
        decay = jnp.where(causal, jnp.exp(jnp.where(causal, diff, 0.0)), 0.0)
        scores = jnp.einsum('bhtk,bhtsk,bhsk->bhts', qb, decay, kb)
        o = jnp.einsum('bhts,bhsv->bhtv', scores, ib) + jnp.einsum('bhtk,bhkv->bhtv', qb * jnp.exp(bcum), S)
        btot = bcum[:, :, -1]
        kdec = kb * jnp.exp(btot[:, :, None, :] - bcum)
        S_new = jnp.exp(btot)[..., None] * S + jnp.einsum('bhsk,bhsv->bhkv', kdec, ib)
        return S_new, o

    S, o = lax.scan(step, s0.astype(jnp.float32), (qc, gc, kc, ic))
    o = o.transpose(1, 0, 3, 2, 4).reshape(B, T, HG_HEADS, HG_DV)
    return o, S


def _mixer(h, s0, lb, w_in, gm_ln_w, gm_ln_b, gm_ws, gm_bs, hg_norm_w, w_branch_a, w_branch_b, w_out):
    B, T, _ = h.shape
    z = h @ w_in
    sizes = (GM_WIDTH, GM_WIDTH, HG_WIDTH, HG_WIDTH, HG_WIDTH, HG_WIDTH, D_MODEL, D_MODEL)
    points = [sum(sizes[:j]) for j in range(1, len(sizes))]
    u, v, q, f, i, g, ga, gb = jnp.split(z, points, axis=-1)
    u = jax.nn.gelu(u, approximate=False)
    v = _layernorm(jax.nn.gelu(v, approximate=False), gm_ln_w, gm_ln_b)
    a = u * _spatial_gate(v, gm_ws, gm_bs)
    qh = jax.nn.silu(q).reshape(B, T, HG_HEADS, HG_DK)
    zf = f.astype(jnp.float32)
    logf = jnp.logaddexp(jax.nn.log_sigmoid(zf), jnp.log(lb.astype(jnp.float32)) + jax.nn.log_sigmoid(-zf))
    logf = logf.reshape(B, T, HG_HEADS, HG_DK)
    ih = i.reshape(B, T, HG_HEADS, HG_DV)
    o, s_new = _hgrn2(qh, logf, ih, s0)
    o = _rmsnorm(o, hg_norm_w).astype(h.dtype)
    o = (o * jax.nn.silu(g).reshape(B, T, HG_HEADS, HG_DV)).reshape(B, T, HG_WIDTH)
    merged = jax.nn.sigmoid(ga) * (a @ w_branch_a) + jax.nn.sigmoid(gb) * (o @ w_branch_b)
    return merged @ w_out, s_new, v


def _swiglu(h, w_gate, w_up, w_down):
    return (jax.nn.silu(h @ w_gate) * (h @ w_up)) @ w_down


def _moe(h, w_router, b_router, w_gate, w_up, w_down):
    logits = (h @ w_router).astype(jnp.float32) + b_router.astype(jnp.float32)
    top_vals, top_idx = lax.top_k(logits, TOP_K)
    wts = jax.nn.softmax(top_vals, axis=-1)
    combine = jnp.sum(jax.nn.one_hot(top_idx, N_EXPERTS, dtype=jnp.float32) * wts[..., None], axis=-2)
    combine = combine.astype(h.dtype)
    out = jnp.zeros_like(h)
    for e in range(N_EXPERTS):
        out = out + combine[..., e:e + 1] * _swiglu(h, w_gate[e], w_up[e], w_down[e])
    return out


def setup_inputs(seed: int = 0) -> dict:
    key = jax.random.key(seed)
    ks = jax.random.split(key, 40)
    nrm = lambda k, shape, s: jax.random.normal(k, shape, jnp.float32) * s
    d = D_MODEL
    return {
        "x_prompt": nrm(ks[0], (BATCH, SEQ, d), 1.0),
        "x_sample": nrm(ks[1], (DEC_BATCH, DEC_SEQ, d), 1.0),
        "c_prompt": nrm(ks[2], (BATCH, d), 1.0),
        "c_sample": nrm(ks[3], (DEC_BATCH, d), 1.0),
        "state_hgrn": nrm(ks[4], (DEPTH, DEC_BATCH, HG_HEADS, HG_DK, HG_DV), 0.5),
        "w_in": nrm(ks[5], (DEPTH, d, IN_COLS), d ** -0.5),
        "gm_ln_w": 1.0 + nrm(ks[6], (DEPTH, GM_WIDTH), 0.02),
        "gm_ln_b": nrm(ks[7], (DEPTH, GM_WIDTH), 0.02),
        "gm_ws": nrm(ks[8], (DEPTH, GM_GROUPS, GM_CHUNK, GM_CHUNK), GM_CHUNK ** -0.5),
        "gm_bs": 1.0 + nrm(ks[9], (DEPTH, GM_GROUPS, GM_CHUNK), 0.02),
        "hg_lb_raw": nrm(ks[10], (DEPTH, HG_WIDTH), 0.5),
        "hg_norm_w": 1.0 + nrm(ks[11], (DEPTH, HG_DV), 0.02),
        "w_branch_a": nrm(ks[12], (DEPTH, GM_WIDTH, d), GM_WIDTH ** -0.5),
        "w_branch_b": nrm(ks[13], (DEPTH, HG_WIDTH, d), HG_WIDTH ** -0.5),
        "w_out": nrm(ks[14], (DEPTH, d, d), d ** -0.5),
        "w_ada": nrm(ks[15], (DEPTH, d, 6 * d), 0.5 * d ** -0.5),
        "b_ada": nrm(ks[16], (DEPTH, 6 * d), 0.02),
        "norm_pre_mix": 1.0 + nrm(ks[17], (DEPTH, d), 0.02),
        "norm_post_mix": 1.0 + nrm(ks[18], (DEPTH, d), 0.02),
        "norm_pre_ffn": 1.0 + nrm(ks[19], (DEPTH, d), 0.02),
        "norm_post_ffn": 1.0 + nrm(ks[20], (DEPTH, d), 0.02),
        "w_ffn_gate": nrm(ks[21], (N_DENSE, d, D_FF), d ** -0.5),
        "w_ffn_up": nrm(ks[22], (N_DENSE, d, D_FF), d ** -0.5),
        "w_ffn_down": nrm(ks[23], (N_DENSE, D_FF, d), D_FF ** -0.5),
        "w_router": nrm(ks[24], (N_MOE, d, N_EXPERTS), d ** -0.5),
        "b_router": nrm(ks[25], (N_MOE, N_EXPERTS), 0.01),
        "w_exp_gate": nrm(ks[26], (N_MOE, N_EXPERTS, d, D_FF_EXPERT), d ** -0.5),
        "w_exp_up": nrm(ks[27], (N_MOE, N_EXPERTS, d, D_FF_EXPERT), d ** -0.5),
        "w_exp_down": nrm(ks[28], (N_MOE, N_EXPERTS, D_FF_EXPERT, d), D_FF_EXPERT ** -0.5),
    }


def reference(x_prompt, x_sample, c_prompt, c_sample, state_hgrn, w_in, gm_ln_w, gm_ln_b, gm_ws, gm_bs,
              hg_lb_raw, hg_norm_w, w_branch_a, w_branch_b, w_out, w_ada, b_ada,
              norm_pre_mix, norm_post_mix, norm_pre_ffn, norm_post_ffn,
              w_ffn_gate, w_ffn_up, w_ffn_down, w_router, b_router, w_exp_gate, w_exp_up, w_exp_down):
    lb = _lower_bounds(hg_lb_raw)

    def run_layer(x, c, s0, l):
        mod = (jax.nn.silu(c) @ w_ada[l] + b_ada[l])[:, None, :]
        sh1, sc1, g1, sh2, sc2, g2 = jnp.split(mod, 6, axis=-1)
        h = _rmsnorm(x, norm_pre_mix[l]) * (1.0 + sc1) + sh1
        mix, s_new, v_rows = _mixer(h, s0, lb[l], w_in[l], gm_ln_w[l], gm_ln_b[l], gm_ws[l], gm_bs[l],
                                    hg_norm_w[l], w_branch_a[l], w_branch_b[l], w_out[l])
        x = x + g1 * _rmsnorm(mix, norm_post_mix[l])
        h = _rmsnorm(x, norm_pre_ffn[l]) * (1.0 + sc2) + sh2
        j = l // 2
        if l % 2 == 0:
            f = _swiglu(h, w_ffn_gate[j], w_ffn_up[j], w_ffn_down[j])
        else:
            f = _moe(h, w_router[j], b_router[j], w_exp_gate[j], w_exp_up[j], w_exp_down[j])
        x = x + g2 * _rmsnorm(f, norm_post_ffn[l])
        return x, s_new, v_rows

    y_prompt, y_sample = x_prompt, x_sample
    s_prompt0 = jnp.zeros((x_prompt.shape[0], HG_HEADS, HG_DK, HG_DV), jnp.float32)
    hp, hs, vs = [], [], []
    for l in range(DEPTH):
        y_prompt, s_p, _v_p = run_layer(y_prompt, c_prompt, s_prompt0, l)
        y_sample, s_s, v_s = run_layer(y_sample, c_sample, state_hgrn[l], l)
        hp.append(s_p.astype(x_prompt.dtype))
        hs.append(s_s.astype(state_hgrn.dtype))
        vs.append(v_s)
    new_state_hgrn_prompt = jnp.stack(hp, axis=0)
    new_state_hgrn_sample = jnp.stack(hs, axis=0)
    new_gmlp_v_sample = jnp.stack(vs, axis=0)
    return (y_prompt, y_sample, new_state_hgrn_prompt, new_state_hgrn_sample, new_gmlp_v_sample)
```

```python
import functools
import math

import jax
import jax.numpy as jnp
from jax import lax
from jax.experimental import pallas as pl
from jax.experimental.pallas import tpu as pltpu

D_MODEL = 1024
BATCH = 8
SEQ = 2048
DEPTH = 4
DEC_BATCH = 128
DEC_SEQ = 8
GM_WIDTH = 1024
GM_GROUPS = 4
GM_GROUP_DIM = GM_WIDTH // GM_GROUPS
GM_CHUNK = 128
HG_HEADS = 8
HG_DK = 128
HG_DV = 128
HG_WIDTH = HG_HEADS * HG_DK
D_FF = 2816
N_EXPERTS = 8
IN_COLS = 2 * GM_WIDTH + 4 * HG_WIDTH + 2 * D_MODEL
EPS = 1e-6

LANES = 128
SUBLANES = 8
TM = 512
N_PROMPT = BATCH * SEQ
N_SAMPLE = DEC_BATCH * DEC_SEQ
N_TOK = N_PROMPT + N_SAMPLE
NT_P = N_PROMPT // TM
NT_S = N_SAMPLE // TM
NT = NT_P + NT_S
TILES_PER_SEQ = SEQ // TM
SEQ_PER_TILE = TM // DEC_SEQ
HG_C = 128
HG_SB = 8
FF_CHUNK = D_FF // 2
EXP_CLAMP = 80.0
VMEM_LIMIT = 56 * 1024 * 1024

_BF = jnp.bfloat16
_F32 = jnp.float32


def _cparams(n_axes):
    return pltpu.CompilerParams(dimension_semantics=("arbitrary",) * n_axes,
                                vmem_limit_bytes=VMEM_LIMIT)


def _dot(a, b):
    return jnp.dot(a, b, preferred_element_type=_F32)


def _silu(x):
    return x * (1.0 / (1.0 + jnp.exp(-x)))


def _sigmoid(x):
    return 1.0 / (1.0 + jnp.exp(-x))


def _gelu(x):
    return 0.5 * x * (1.0 + lax.erf(x * (1.0 / math.sqrt(2.0))))


def _softplus(x):
    return jnp.maximum(x, 0.0) + jnp.log1p(jnp.exp(-jnp.abs(x)))


def _rms(x, w):
    return x * lax.rsqrt(jnp.mean(x * x, axis=-1, keepdims=True) + EPS) * w


def _is_sample_tile():
    return pl.program_id(0) >= NT_P


def _mod_rows(mp_ref, ms_ref):
    mp = jnp.broadcast_to(mp_ref[...], ms_ref.shape)
    return jnp.where(_is_sample_tile(), ms_ref[...], mp)


def _per_seq(x, fn_rows):
    x3 = x.reshape(SEQ_PER_TILE, DEC_SEQ, x.shape[-1])
    return fn_rows(x3).reshape(TM, x.shape[-1])


def _mod_specs(layer, col_block):
    mp = pl.BlockSpec((None, None, 1, D_MODEL),
                      lambda i: (layer, jnp.minimum(i // TILES_PER_SEQ, BATCH - 1), 0, col_block))
    ms = pl.BlockSpec((None, SEQ_PER_TILE, D_MODEL),
                      lambda i: (layer, jnp.maximum(i - NT_P, 0), col_block))
    return [mp, ms]


def _tile_spec(width=D_MODEL):
    return pl.BlockSpec((TM, width), lambda i: (i, 0))


def _row_spec(layer, width=D_MODEL):
    return pl.BlockSpec((None, 1, width), lambda i: (layer, 0, 0))


def _ada_kernel(c_ref, w_ref, b_ref, o_ref):
    s = _silu(c_ref[...]).astype(_BF)
    o_ref[...] = _dot(s, w_ref[...].astype(_BF)) + b_ref[...]


def _ada_call(c_all, w_ada, b_ada):
    nb = 4
    wcol = 6 * D_MODEL // nb
    rows = c_all.shape[0]
    return pl.pallas_call(
        _ada_kernel,
        grid=(DEPTH, nb),
        in_specs=[pl.BlockSpec((rows, D_MODEL), lambda l, j: (0, 0)),
                  pl.BlockSpec((None, D_MODEL, wcol), lambda l, j: (l, 0, j)),
                  pl.BlockSpec((None, 1, wcol), lambda l, j: (l, 0, j))],
        out_specs=pl.BlockSpec((None, rows, wcol), lambda l, j: (l, 0, j)),
        out_shape=jax.ShapeDtypeStruct((DEPTH, rows, 6 * D_MODEL), _F32),
        compiler_params=_cparams(2),
        name="ada",
    )(c_all, w_ada, b_ada.reshape(DEPTH, 1, 6 * D_MODEL))


def _lb_kernel(raw_ref, loglb_ref):
    raw = raw_ref[...]
    m = jnp.max(raw, axis=0, keepdims=True)
    e = jnp.exp(raw - m)
    p = e / jnp.sum(e, axis=0, keepdims=True)
    acc = jnp.zeros((1, HG_WIDTH), _F32)
    for l in range(DEPTH):
        if l > 0:
            acc = acc + p[l:l + 1, :]
        loglb_ref[l:l + 1, :] = jnp.log(acc)


def _lb_call(raw):
    return pl.pallas_call(
        _lb_kernel,
        out_shape=jax.ShapeDtypeStruct((DEPTH, HG_WIDTH), _F32),
        name="lower_bounds",
    )(raw)


def _inproj_kernel(kinds, x_ref, mp_sh, ms_sh, mp_sc, ms_sc, nw_ref, w_ref, aux_ref, *out_refs):
    x = x_ref[...]
    xn = _rms(x, nw_ref[...])
    sh = _mod_rows(mp_sh, ms_sh)
    sc = _mod_rows(mp_sc, ms_sc)
    h = _per_seq(xn, lambda x3: x3 * (1.0 + sc)[:, None, :] + sh[:, None, :]).astype(_BF)
    outs = list(out_refs)
    for j, kind in enumerate(kinds):
        z = _dot(h, w_ref[:, j * D_MODEL:(j + 1) * D_MODEL])
        if kind == "gelu":
            outs.pop(0)[...] = _gelu(z).astype(_BF)
        elif kind == "gelu_ln":
            a = _gelu(z)
            mu = jnp.mean(a, axis=-1, keepdims=True)
            ac = a - mu
            vn = ac * lax.rsqrt(jnp.mean(ac * ac, axis=-1, keepdims=True) + EPS)
            vn = vn * aux_ref[0:1, :] + aux_ref[1:2, :]
            outs.pop(0)[...] = vn.astype(_BF)
            v32_ref = outs.pop(0)

            @pl.when(_is_sample_tile())
            def _():
                v32_ref[...] = vn
        elif kind == "silu":
            outs.pop(0)[...] = _silu(z).astype(_BF)
        elif kind == "sigmoid":
            outs.pop(0)[...] = _sigmoid(z).astype(_BF)
        elif kind == "id":
            outs.pop(0)[...] = z.astype(_BF)
        elif kind == "logf":
            a = -_softplus(-z)
            c = aux_ref[2:3, :] - _softplus(z)
            outs.pop(0)[...] = jnp.maximum(a, c) + jnp.log1p(jnp.exp(-jnp.abs(a - c)))
        else:
            raise ValueError(kind)


def _inproj_call(layer, sec0, kinds, x, modp, mods, norm_w, w_in_bf, aux):
    nsec = len(kinds)
    out_specs, out_shape = [], []
    for kind in kinds:
        dt = _F32 if kind == "logf" else _BF
        out_specs.append(_tile_spec())
        out_shape.append(jax.ShapeDtypeStruct((N_TOK, D_MODEL), dt))
        if kind == "gelu_ln":
            out_specs.append(pl.BlockSpec((TM, D_MODEL), lambda i: (jnp.maximum(i - NT_P, 0), 0)))
            out_shape.append(jax.ShapeDtypeStruct((N_SAMPLE, D_MODEL), _F32))
    return pl.pallas_call(
        functools.partial(_inproj_kernel, kinds),
        grid=(NT,),
        in_specs=[_tile_spec()] + _mod_specs(layer, 0) + _mod_specs(layer, 1) + [
            _row_spec(layer),
            pl.BlockSpec((None, D_MODEL, nsec * D_MODEL), lambda i: (layer, 0, sec0 // nsec)),
            pl.BlockSpec((None, SUBLANES, D_MODEL), lambda i: (layer, 0, 0))],
        out_specs=out_specs,
        out_shape=out_shape,
        compiler_params=_cparams(1),
        name=f"inproj_{sec0}",
    )(x, modp, mods, modp, mods, norm_w, w_in_bf, aux)


def _col_bcast(row):
    n = row.shape[-1]
    return jnp.broadcast_to(row, (n, n)).T


def _head_out(o, nw, sg):
    return (_rms(o, nw) * sg.astype(_F32)).astype(_BF)


def _hgrn_prompt_kernel(q_ref, g_ref, i_ref, sg_ref, nw_ref, o_ref, s_out_ref, s_scr):
    c = pl.program_id(1)

    @pl.when(c == 0)
    def _():
        s_scr[...] = jnp.zeros_like(s_scr)

    g = g_ref[...]
    row = lax.broadcasted_iota(jnp.int32, (HG_C, HG_C), 0)
    col = lax.broadcasted_iota(jnp.int32, (HG_C, HG_C), 1)
    causal = row >= col
    tril = causal.astype(_BF)
    g_hi = g.astype(_BF)
    r1 = g - g_hi.astype(_F32)
    g_mid = r1.astype(_BF)
    g_lo = (r1 - g_mid.astype(_F32)).astype(_BF)
    b = _dot(tril, g_hi) + _dot(tril, g_mid) + _dot(tril, g_lo)
    kk = 1.0 - jnp.exp(g)
    q = q_ref[...].astype(_F32)
    ref = b[HG_C // 2 - 1:HG_C // 2, :]
    bend = b[HG_C - 1:HG_C, :]
    qt = (q * jnp.exp(jnp.minimum(b - ref, EXP_CLAMP))).astype(_BF)
    kt = (kk * jnp.exp(jnp.minimum(ref - b, EXP_CLAMP))).astype(_BF)
    qe = (q * jnp.exp(b)).astype(_BF)
    kdec = (kk * jnp.exp(bend - b)).astype(_BF)
    e_end = jnp.exp(bend)
    for h in range(HG_HEADS):
        hs = slice(h * HG_DK, (h + 1) * HG_DK)
        iv = i_ref[:, hs]
        scores = lax.dot_general(qt[:, hs], kt[:, hs], (((1,), (1,)), ((), ())),
                                 preferred_element_type=_F32)
        scores = jnp.where(causal, scores, 0.0).astype(_BF)
        s_old = s_scr[h]
        o = _dot(scores, iv) + _dot(qe[:, hs], s_old.astype(_BF))
        upd = lax.dot_general(kdec[:, hs], iv, (((0,), (0,)), ((), ())),
                              preferred_element_type=_F32)
        s_scr[h] = _col_bcast(e_end[:, hs]) * s_old + upd
        o_ref[:, hs] = _head_out(o, nw_ref[...], sg_ref[:, hs])

    @pl.when(c == pl.num_programs(1) - 1)
    def _():
        s_out_ref[...] = s_scr[...]


def _hgrn_prompt_call(layer, q, g, iv, sg, hg_norm_w):
    nc = SEQ // HG_C
    blk = pl.BlockSpec((HG_C, HG_WIDTH), lambda b, c: (b * nc + c, 0))
    return pl.pallas_call(
        _hgrn_prompt_kernel,
        grid=(BATCH, nc),
        in_specs=[blk, blk, blk, blk,
                  pl.BlockSpec((None, 1, HG_DV), lambda b, c: (layer, 0, 0))],
        out_specs=[blk,
                   pl.BlockSpec((None, HG_HEADS, HG_DK, HG_DV), lambda b, c: (b, 0, 0, 0))],
        out_shape=[jax.ShapeDtypeStruct((N_PROMPT, HG_WIDTH), _BF),
                   jax.ShapeDtypeStruct((BATCH, HG_HEADS, HG_DK, HG_DV), _F32)],
        scratch_shapes=[pltpu.VMEM((HG_HEADS, HG_DK, HG_DV), _F32)],
        compiler_params=_cparams(2),
        name="hgrn_prompt",
    )(q, g, iv, sg, hg_norm_w)


def _hgrn_sample_kernel(q_ref, g_ref, i_ref, sg_ref, nw_ref, s_in_ref, o_ref, s_out_ref):
    t_row = lax.broadcasted_iota(jnp.int32, (DEC_SEQ, HG_WIDTH), 0)
    t_row_h = lax.broadcasted_iota(jnp.int32, (DEC_SEQ, HG_DK), 0)

    def seq_body(s, carry):
        rows = pl.ds(pl.multiple_of(s * DEC_SEQ, DEC_SEQ), DEC_SEQ)
        g = g_ref[rows, :]
        b = g
        shift = 1
        while shift < DEC_SEQ:
            b = b + jnp.where(t_row >= shift, pltpu.roll(b, shift, axis=0), 0.0)
            shift *= 2
        kk = 1.0 - jnp.exp(g)
        q = q_ref[rows, :].astype(_F32)
        bend = b[DEC_SEQ - 1:DEC_SEQ, :]
        qe = (q * jnp.exp(b)).astype(_BF)
        kdec = (kk * jnp.exp(bend - b)).astype(_BF)
        e_end = jnp.exp(bend)
        iv_all = i_ref[rows, :]
        sg_all = sg_ref[rows, :]
        for h in range(HG_HEADS):
            hs = slice(h * HG_DK, (h + 1) * HG_DK)
            qh, bh, kh = q[:, hs], b[:, hs], kk[:, hs]
            iv = iv_all[:, hs]
            ivf = iv.astype(_F32)
            s_old = s_in_ref[s, h]
            o = _dot(qe[:, hs], s_old.astype(_BF))
            for j in range(DEC_SEQ):
                decay = jnp.exp(jnp.minimum(bh - bh[j:j + 1, :], 0.0))
                w = jnp.sum(qh * decay * kh[j:j + 1, :], axis=-1, keepdims=True)
                w = jnp.where(t_row_h[:, 0:1] >= j, w, 0.0)
                o = o + w * ivf[j:j + 1, :]
            upd = lax.dot_general(kdec[:, hs], iv, (((0,), (0,)), ((), ())),
                                  preferred_element_type=_F32)
            s_out_ref[s, h] = _col_bcast(e_end[:, hs]) * s_old + upd
            o_ref[rows, hs] = _head_out(o, nw_ref[...], sg_all[:, hs])
        return carry

    lax.fori_loop(0, HG_SB, seq_body, 0)


def _hgrn_sample_call(layer, q, g, iv, sg, hg_norm_w, state):
    rows = HG_SB * DEC_SEQ
    off = N_PROMPT // rows
    blk = pl.BlockSpec((rows, HG_WIDTH), lambda j: (off + j, 0))
    return pl.pallas_call(
        _hgrn_sample_kernel,
        grid=(DEC_BATCH // HG_SB,),
        in_specs=[blk, blk, blk, blk,
                  pl.BlockSpec((None, 1, HG_DV), lambda j: (layer, 0, 0)),
                  pl.BlockSpec((None, HG_SB, HG_HEADS, HG_DK, HG_DV), lambda j: (layer, j, 0, 0, 0))],
        out_specs=[pl.BlockSpec((rows, HG_WIDTH), lambda j: (j, 0)),
                   pl.BlockSpec((HG_SB, HG_HEADS, HG_DK, HG_DV), lambda j: (j, 0, 0, 0))],
        out_shape=[jax.ShapeDtypeStruct((N_SAMPLE, HG_WIDTH), _BF),
                   jax.ShapeDtypeStruct((DEC_BATCH, HG_HEADS, HG_DK, HG_DV), _F32)],
        compiler_params=_cparams(1),
        name="hgrn_sample",
    )(q, g, iv, sg, hg_norm_w, state)


def _mix_kernel(with_router, u_ref, v_ref, sga_ref, sgb_ref, op_ref, os_ref, x_ref,
                mp_g1, ms_g1, mp_sh, ms_sh, mp_sc, ms_sc,
                wmix_ref, bmix_ref, wa_ref, wb_ref, wo_ref, npost_ref, npre_ref, *rest):
    if with_router:
        wr_ref, br_ref, x1_ref, h2_ref, comb_ref, a_scr = rest
    else:
        x1_ref, h2_ref, a_scr = rest
    for c in range(TM // GM_CHUNK):
        rows = slice(c * GM_CHUNK, (c + 1) * GM_CHUNK)
        for gi in range(GM_GROUPS):
            cols = slice(gi * GM_GROUP_DIM, (gi + 1) * GM_GROUP_DIM)
            mixed = _dot(wmix_ref[gi], v_ref[rows, cols])
            bias = bmix_ref[gi]
            mixed = mixed + jnp.concatenate([bias] * (GM_GROUP_DIM // LANES), axis=1)
            a_scr[rows, cols] = (u_ref[rows, cols].astype(_F32) * mixed).astype(_BF)
    br_a = _dot(a_scr[...], wa_ref[...])
    o = jnp.where(_is_sample_tile(), os_ref[...], op_ref[...])
    br_b = _dot(o, wb_ref[...])
    merged = sga_ref[...].astype(_F32) * br_a + sgb_ref[...].astype(_F32) * br_b
    mix = _dot(merged.astype(_BF), wo_ref[...])
    g1 = _mod_rows(mp_g1, ms_g1)
    sh = _mod_rows(mp_sh, ms_sh)
    sc = _mod_rows(mp_sc, ms_sc)
    nm = _rms(mix, npost_ref[...])
    x1 = x_ref[...] + _per_seq(nm, lambda t: t * g1[:, None, :])
    x1_ref[...] = x1
    xn = _rms(x1, npre_ref[...])
    h2 = _per_seq(xn, lambda t: t * (1.0 + sc)[:, None, :] + sh[:, None, :])
    h2_ref[...] = h2.astype(_BF)
    if with_router:
        logits = jnp.dot(h2, wr_ref[...], preferred_element_type=_F32,
                         precision=lax.Precision.HIGHEST) + br_ref[...]
        lane = lax.broadcasted_iota(jnp.int32, logits.shape, 1)
        neg = jnp.float32(-jnp.inf)
        logits = jnp.where(lane < N_EXPERTS, logits, neg)
        m1 = jnp.max(logits, axis=-1, keepdims=True)
        i1 = jnp.min(jnp.where(logits == m1, lane, LANES), axis=-1, keepdims=True)
        rest_l = jnp.where(lane == i1, neg, logits)
        m2 = jnp.max(rest_l, axis=-1, keepdims=True)
        i2 = jnp.min(jnp.where(rest_l == m2, lane, LANES), axis=-1, keepdims=True)
        e2 = jnp.exp(m2 - m1)
        w1 = 1.0 / (1.0 + e2)
        w2 = e2 / (1.0 + e2)
        comb_ref[...] = jnp.where(lane == i1, w1, 0.0) + jnp.where(lane == i2, w2, 0.0)


def _mix_call(layer, moe_idx, u, v, sga, sgb, o_p, o_s, x, modp, mods, wmix, bmix,
              wa, wb, wo, npost, npre, wr=None, br=None):
    with_router = moe_idx is not None
    ty = lambda i: (i >= NT_P).astype(jnp.int32)
    wspec = pl.BlockSpec((None, D_MODEL, D_MODEL), lambda i: (layer, 0, 0))
    in_specs = [_tile_spec(), _tile_spec(), _tile_spec(), _tile_spec(),
                pl.BlockSpec((TM, HG_WIDTH), lambda i: (jnp.minimum(i, NT_P - 1), 0)),
                pl.BlockSpec((TM, HG_WIDTH), lambda i: (jnp.maximum(i - NT_P, 0), 0)),
                _tile_spec()]
    in_specs += _mod_specs(layer, 2) + _mod_specs(layer, 3) + _mod_specs(layer, 4)
    in_specs += [pl.BlockSpec((None, None, GM_GROUPS, GM_CHUNK, GM_CHUNK), lambda i: (layer, ty(i), 0, 0, 0)),
                 pl.BlockSpec((None, None, GM_GROUPS, GM_CHUNK, LANES), lambda i: (layer, ty(i), 0, 0, 0)),
                 wspec, wspec, wspec, _row_spec(layer), _row_spec(layer)]
    args = [u, v, sga, sgb, o_p, o_s, x, modp, mods, modp, mods, modp, mods,
            wmix, bmix, wa, wb, wo, npost, npre]
    out_specs = [_tile_spec(), _tile_spec()]
    out_shape = [jax.ShapeDtypeStruct((N_TOK, D_MODEL), _F32),
                 jax.ShapeDtypeStruct((N_TOK, D_MODEL), _BF)]
    if with_router:
        in_specs += [pl.BlockSpec((None, D_MODEL, LANES), lambda i: (moe_idx, 0, 0)),
                     pl.BlockSpec((None, 1, LANES), lambda i: (moe_idx, 0, 0))]
        args += [wr, br]
        out_specs.append(_tile_spec(LANES))
        out_shape.append(jax.ShapeDtypeStruct((N_TOK, LANES), _F32))
    return pl.pallas_call(
        functools.partial(_mix_kernel, with_router),
        grid=(NT,),
        in_specs=in_specs,
        out_specs=out_specs,
        out_shape=out_shape,
        scratch_shapes=[pltpu.VMEM((TM, GM_WIDTH), _BF)],
        compiler_params=_cparams(1),
        name="mix",
    )(*args)


def _ffn_partial(h, wg_ref, wu_ref, wd_ref):
    gate = _dot(h, wg_ref[...])
    up = _dot(h, wu_ref[...])
    act = (_silu(gate) * up).astype(_BF)
    return _dot(act, wd_ref[...])


def _residual_out(x1_ref, f, mp_g2, ms_g2, npost_ref, out_ref):
    g2 = _mod_rows(mp_g2, ms_g2)
    nf = _rms(f, npost_ref[...])
    out_ref[...] = x1_ref[...] + _per_seq(nf, lambda t: t * g2[:, None, :])


def _ffn_kernel(h_ref, x1_ref, mp_g2, ms_g2, wg_ref, wu_ref, wd_ref, npost_ref, out_ref, acc_ref):
    k = pl.program_id(1)
    part = _ffn_partial(h_ref[...], wg_ref, wu_ref, wd_ref)

    @pl.when(k == 0)
    def _():
        acc_ref[...] = part

    @pl.when(k > 0)
    def _():
        acc_ref[...] += part

    @pl.when(k == pl.num_programs(1) - 1)
    def _():
        _residual_out(x1_ref, acc_ref[...], mp_g2, ms_g2, npost_ref, out_ref)


def _mod_specs2(layer, col_block):
    mp, ms = _mod_specs(layer, col_block)
    return [pl.BlockSpec(mp.block_shape, lambda i, *_: mp.index_map(i)),
            pl.BlockSpec(ms.block_shape, lambda i, *_: ms.index_map(i))]


def _ffn_call(layer, j, h2, x1, modp, mods, wg, wu, wd, npost):
    nk = D_FF // FF_CHUNK
    tile = pl.BlockSpec((TM, D_MODEL), lambda i, k: (i, 0))
    return pl.pallas_call(
        _ffn_kernel,
        grid=(NT, nk),
        in_specs=[tile, tile] + _mod_specs2(layer, 5) + [
            pl.BlockSpec((None, D_MODEL, FF_CHUNK), lambda i, k: (j, 0, k)),
            pl.BlockSpec((None, D_MODEL, FF_CHUNK), lambda i, k: (j, 0, k)),
            pl.BlockSpec((None, FF_CHUNK, D_MODEL), lambda i, k: (j, k, 0)),
            pl.BlockSpec((None, 1, D_MODEL), lambda i, k: (layer, 0, 0))],
        out_specs=tile,
        out_shape=jax.ShapeDtypeStruct((N_TOK, D_MODEL), _F32),
        scratch_shapes=[pltpu.VMEM((TM, D_MODEL), _F32)],
        compiler_params=_cparams(2),
        name="ffn",
    )(h2, x1, modp, mods, wg, wu, wd, npost)


def _moe_kernel(h_ref, x1_ref, comb_ref, mp_g2, ms_g2, wg_ref, wu_ref, wd_ref, npost_ref,
                out_ref, acc_ref):
    e = pl.program_id(1)
    k = pl.program_id(2)
    part = _ffn_partial(h_ref[...], wg_ref, wu_ref, wd_ref)
    comb = comb_ref[...]
    lane = lax.broadcasted_iota(jnp.int32, comb.shape, 1)
    w = jnp.sum(jnp.where(lane == e, comb, 0.0), axis=-1, keepdims=True)
    first = jnp.logical_and(e == 0, k == 0)

    @pl.when(first)
    def _():
        acc_ref[...] = w * part

    @pl.when(jnp.logical_not(first))
    def _():
        acc_ref[...] += w * part

    last = jnp.logical_and(e == pl.num_programs(1) - 1, k == pl.num_programs(2) - 1)

    @pl.when(last)
    def _():
        _residual_out(x1_ref, acc_ref[...], mp_g2, ms_g2, npost_ref, out_ref)


def _moe_call(layer, j, h2, x1, comb, modp, mods, wg, wu, wd, npost):
    nk = D_FF // FF_CHUNK
    tile = pl.BlockSpec((TM, D_MODEL), lambda i, e, k: (i, 0))
    return pl.pallas_call(
        _moe_kernel,
        grid=(NT, N_EXPERTS, nk),
        in_specs=[tile, tile, pl.BlockSpec((TM, LANES), lambda i, e, k: (i, 0))]
        + _mod_specs2(layer, 5) + [
            pl.BlockSpec((None, None, D_MODEL, FF_CHUNK), lambda i, e, k: (j, e, 0, k)),
            pl.BlockSpec((None, None, D_MODEL, FF_CHUNK), lambda i, e, k: (j, e, 0, k)),
            pl.BlockSpec((None, None, FF_CHUNK, D_MODEL), lambda i, e, k: (j, e, k, 0)),
            pl.BlockSpec((None, 1, D_MODEL), lambda i, e, k: (layer, 0, 0))],
        out_specs=tile,
        out_shape=jax.ShapeDtypeStruct((N_TOK, D_MODEL), _F32),
        scratch_shapes=[pltpu.VMEM((TM, D_MODEL), _F32)],
        compiler_params=_cparams(3),
        name="moe",
    )(h2, x1, comb, modp, mods, wg, wu, wd, npost)


def _spatial_tables(gm_ws, gm_bs):
    mask_p = jnp.tril(jnp.ones((GM_CHUNK, GM_CHUNK), bool))
    w_p = jnp.where(mask_p, gm_ws, 0.0)
    mask_s = jnp.tril(jnp.ones((DEC_SEQ, DEC_SEQ), bool))
    w_small = jnp.where(mask_s, gm_ws[:, :, :DEC_SEQ, :DEC_SEQ], 0.0)
    eye = jnp.eye(GM_CHUNK // DEC_SEQ, dtype=gm_ws.dtype)
    w_s = jnp.einsum("ab,lgts->lgatbs", eye, w_small).reshape(gm_ws.shape)
    b_p = gm_bs
    b_s = jnp.tile(gm_bs[:, :, :DEC_SEQ], (1, 1, GM_CHUNK // DEC_SEQ))
    wmix = jnp.stack([w_p, w_s], axis=1).astype(_BF)
    bmix = jnp.stack([b_p, b_s], axis=1)[..., None]
    bmix = jnp.broadcast_to(bmix, bmix.shape[:-1] + (LANES,)).astype(_F32)
    return wmix, bmix


def kernel(x_prompt, x_sample, c_prompt, c_sample, state_hgrn, w_in, gm_ln_w, gm_ln_b, gm_ws, gm_bs,
           hg_lb_raw, hg_norm_w, w_branch_a, w_branch_b, w_out, w_ada, b_ada,
           norm_pre_mix, norm_post_mix, norm_pre_ffn, norm_post_ffn,
           w_ffn_gate, w_ffn_up, w_ffn_down, w_router, b_router, w_exp_gate, w_exp_up, w_exp_down):
    x = jnp.concatenate([x_prompt.reshape(N_PROMPT, D_MODEL), x_sample.reshape(N_SAMPLE, D_MODEL)], axis=0)
    c_all = jnp.concatenate([c_prompt, c_sample], axis=0)
    mod = _ada_call(c_all, w_ada, b_ada)
    modp = mod[:, :BATCH].reshape(DEPTH, BATCH, 1, 6 * D_MODEL)
    mods = mod[:, BATCH:]
    loglb = _lb_call(hg_lb_raw)
    aux = jnp.zeros((DEPTH, SUBLANES, D_MODEL), _F32)
    aux = aux.at[:, 0].set(gm_ln_w).at[:, 1].set(gm_ln_b).at[:, 2].set(loglb)
    wmix, bmix = _spatial_tables(gm_ws, gm_bs)
    row3 = lambda a: a.reshape(a.shape[0], 1, a.shape[1])
    w_in_bf = w_in.astype(_BF)
    wa_bf, wb_bf, wo_bf = w_branch_a.astype(_BF), w_branch_b.astype(_BF), w_out.astype(_BF)
    wfg, wfu, wfd = w_ffn_gate.astype(_BF), w_ffn_up.astype(_BF), w_ffn_down.astype(_BF)
    weg, weu, wed = w_exp_gate.astype(_BF), w_exp_up.astype(_BF), w_exp_down.astype(_BF)
    wr_pad = jnp.pad(w_router, ((0, 0), (0, 0), (0, LANES - N_EXPERTS)))
    br_pad = jnp.pad(b_router, ((0, 0), (0, LANES - N_EXPERTS))).reshape(-1, 1, LANES)
    npre_mix, npost_mix = row3(norm_pre_mix), row3(norm_post_mix)
    npre_ffn, npost_ffn = row3(norm_pre_ffn), row3(norm_post_ffn)
    hg_nw = row3(hg_norm_w)

    hp, hs, vs = [], [], []
    for l in range(DEPTH):
        u, v, v32, q, g = _inproj_call(l, 0, ("gelu", "gelu_ln", "silu", "logf"),
                                       x, modp, mods, npre_mix, w_in_bf, aux)
        iv, sg, sga, sgb = _inproj_call(l, 4, ("id", "silu", "sigmoid", "sigmoid"),
                                        x, modp, mods, npre_mix, w_in_bf, aux)
        o_p, s_p = _hgrn_prompt_call(l, q, g, iv, sg, hg_nw)
        o_s, s_s = _hgrn_sample_call(l, q, g, iv, sg, hg_nw, state_hgrn)
        j = l // 2
        if l % 2 == 0:
            x1, h2 = _mix_call(l, None, u, v, sga, sgb, o_p, o_s, x, modp, mods, wmix, bmix,
                               wa_bf, wb_bf, wo_bf, npost_mix, npre_ffn)
            x = _ffn_call(l, j, h2, x1, modp, mods, wfg, wfu, wfd, npost_ffn)
        else:
            x1, h2, comb = _mix_call(l, j, u, v, sga, sgb, o_p, o_s, x, modp, mods, wmix, bmix,
                                     wa_bf, wb_bf, wo_bf, npost_mix, npre_ffn, wr_pad, br_pad)
            x = _moe_call(l, j, h2, x1, comb, modp, mods, weg, weu, wed, npost_ffn)
        hp.append(s_p)
        hs.append(s_s)
        vs.append(v32.reshape(DEC_BATCH, DEC_SEQ, GM_WIDTH))
    y_prompt = x[:N_PROMPT].reshape(BATCH, SEQ, D_MODEL)
    y_sample = x[N_PROMPT:].reshape(DEC_BATCH, DEC_SEQ, D_MODEL)
    return (y_prompt, y_sample, jnp.stack(hp, axis=0), jnp.stack(hs, axis=0), jnp.stack(vs, axis=0))
```

```python
import functools
import math

import jax
import jax.numpy as jnp
from jax import lax
from jax.experimental import pallas as pl
from jax.experimental.pallas import tpu as pltpu

D_MODEL = 1024
BATCH = 8
SEQ = 2048
DEPTH = 4
DEC_BATCH = 128
DEC_SEQ = 8
GM_WIDTH = 1024
GM_GROUPS = 4
GM_GROUP_DIM = GM_WIDTH // GM_GROUPS
GM_CHUNK = 128
HG_HEADS = 8
HG_DK = 128
HG_DV = 128
HG_WIDTH = HG_HEADS * HG_DK
D_FF = 2816
N_EXPERTS = 8
TOP_K = 2
IN_COLS = 2 * GM_WIDTH + 4 * HG_WIDTH + 2 * D_MODEL
EPS = 1e-6

LANES = 128
SUBLANES = 8
BF16_ROWS = 16
TM = 512
N_PROMPT = BATCH * SEQ
N_SAMPLE = DEC_BATCH * DEC_SEQ
N_TOK = N_PROMPT + N_SAMPLE
NT_P = N_PROMPT // TM
NT_S = N_SAMPLE // TM
NT = NT_P + NT_S
TILES_PER_SEQ = SEQ // TM
SEQ_PER_TILE = TM // DEC_SEQ
HG_C = 64
HG_BLK = 256
HG_SB = 8
FF_CHUNK = D_FF // 2
EXP_RANGE = 60.0
VMEM_LIMIT = 56 * 1024 * 1024

TG = 512
RUN_ALIGN = BF16_ROWS
RUN_SIZES = tuple(TM >> k for k in range((TM // RUN_ALIGN).bit_length()))
CB = TOP_K * TM + N_EXPERTS * RUN_ALIGN
_ROWS_BOUND = TOP_K * N_TOK + NT * N_EXPERTS * (RUN_ALIGN - 1) + N_EXPERTS * (TG - RUN_ALIGN)
R_TILES = -(-_ROWS_BOUND // TG)
N_ROWS = R_TILES * TG

_BF = jnp.bfloat16
_F32 = jnp.float32
_I32 = jnp.int32


def _cparams(n_axes):
    return pltpu.CompilerParams(dimension_semantics=("arbitrary",) * n_axes,
                                vmem_limit_bytes=VMEM_LIMIT)


def _dot(a, b):
    return jnp.dot(a, b, preferred_element_type=_F32)


def _silu(x):
    return x * (1.0 / (1.0 + jnp.exp(-x)))


def _sigmoid(x):
    return 1.0 / (1.0 + jnp.exp(-x))


def _gelu(x):
    return 0.5 * x * (1.0 + lax.erf(x * (1.0 / math.sqrt(2.0))))


def _softplus(x):
    return jnp.maximum(x, 0.0) + jnp.log1p(jnp.exp(-jnp.abs(x)))


def _rms(x, w):
    return x * lax.rsqrt(jnp.mean(x * x, axis=-1, keepdims=True) + EPS) * w


def _is_sample_tile():
    return pl.program_id(0) >= NT_P


def _mod_rows(mp_ref, ms_ref):
    mp = jnp.broadcast_to(mp_ref[...], ms_ref.shape)
    return jnp.where(_is_sample_tile(), ms_ref[...], mp)


def _per_seq(x, fn_rows):
    x3 = x.reshape(SEQ_PER_TILE, DEC_SEQ, x.shape[-1])
    return fn_rows(x3).reshape(TM, x.shape[-1])


def _mod_specs(layer, col_block):
    mp = pl.BlockSpec((None, None, 1, D_MODEL),
                      lambda i, *_: (layer, jnp.minimum(i // TILES_PER_SEQ, BATCH - 1), 0, col_block))
    ms = pl.BlockSpec((None, SEQ_PER_TILE, D_MODEL),
                      lambda i, *_: (layer, jnp.maximum(i - NT_P, 0), col_block))
    return [mp, ms]


def _tile_spec(width=D_MODEL):
    return pl.BlockSpec((TM, width), lambda i, *_: (i, 0))


def _row_spec(layer, width=D_MODEL):
    return pl.BlockSpec((None, 1, width), lambda i, *_: (layer, 0, 0))


def _ada_kernel(c_ref, w_ref, b_ref, o_ref):
    s = _silu(c_ref[...]).astype(_BF)
    o_ref[...] = _dot(s, w_ref[...].astype(_BF)) + b_ref[...]


def _ada_call(c_all, w_ada, b_ada):
    nb = 4
    wcol = 6 * D_MODEL // nb
    rows = c_all.shape[0]
    return pl.pallas_call(
        _ada_kernel,
        grid=(DEPTH, nb),
        in_specs=[pl.BlockSpec((rows, D_MODEL), lambda l, j: (0, 0)),
                  pl.BlockSpec((None, D_MODEL, wcol), lambda l, j: (l, 0, j)),
                  pl.BlockSpec((None, 1, wcol), lambda l, j: (l, 0, j))],
        out_specs=pl.BlockSpec((None, rows, wcol), lambda l, j: (l, 0, j)),
        out_shape=jax.ShapeDtypeStruct((DEPTH, rows, 6 * D_MODEL), _F32),
        compiler_params=_cparams(2),
        name="ada",
    )(c_all, w_ada, b_ada.reshape(DEPTH, 1, 6 * D_MODEL))


def _lb_kernel(raw_ref, loglb_ref):
    raw = raw_ref[...]
    m = jnp.max(raw, axis=0, keepdims=True)
    e = jnp.exp(raw - m)
    p = e / jnp.sum(e, axis=0, keepdims=True)
    acc = jnp.zeros((1, HG_WIDTH), _F32)
    for l in range(DEPTH):
        if l > 0:
            acc = acc + p[l:l + 1, :]
        loglb_ref[l:l + 1, :] = jnp.log(acc)


def _lb_call(raw):
    return pl.pallas_call(
        _lb_kernel,
        out_shape=jax.ShapeDtypeStruct((DEPTH, HG_WIDTH), _F32),
        name="lower_bounds",
    )(raw)


def _inproj_kernel(kinds, x_ref, mp_sh, ms_sh, mp_sc, ms_sc, nw_ref, w_ref, aux_ref, *out_refs):
    x = x_ref[...]
    xn = _rms(x, nw_ref[...])
    sh = _mod_rows(mp_sh, ms_sh)
    sc = _mod_rows(mp_sc, ms_sc)
    h = _per_seq(xn, lambda x3: x3 * (1.0 + sc)[:, None, :] + sh[:, None, :]).astype(_BF)
    outs = list(out_refs)
    for j, kind in enumerate(kinds):
        z = _dot(h, w_ref[:, j * D_MODEL:(j + 1) * D_MODEL])
        if kind == "gelu":
            outs.pop(0)[...] = _gelu(z).astype(_BF)
        elif kind == "gelu_ln":
            a = _gelu(z)
            mu = jnp.mean(a, axis=-1, keepdims=True)
            ac = a - mu
            vn = ac * lax.rsqrt(jnp.mean(ac * ac, axis=-1, keepdims=True) + EPS)
            vn = vn * aux_ref[0:1, :] + aux_ref[1:2, :]
            outs.pop(0)[...] = vn.astype(_BF)
            v32_ref = outs.pop(0)

            @pl.when(_is_sample_tile())
            def _():
                v32_ref[...] = vn
        elif kind == "silu":
            outs.pop(0)[...] = _silu(z).astype(_BF)
        elif kind == "sigmoid":
            outs.pop(0)[...] = _sigmoid(z).astype(_BF)
        elif kind == "id":
            outs.pop(0)[...] = z.astype(_BF)
        elif kind == "logf":
            a = -_softplus(-z)
            c = aux_ref[2:3, :] - _softplus(z)
            outs.pop(0)[...] = jnp.maximum(a, c) + jnp.log1p(jnp.exp(-jnp.abs(a - c)))
        else:
            raise ValueError(kind)


def _inproj_call(layer, sec0, kinds, x, modp, mods, norm_w, w_in_bf, aux):
    nsec = len(kinds)
    out_specs, out_shape = [], []
    for kind in kinds:
        dt = _F32 if kind == "logf" else _BF
        out_specs.append(_tile_spec())
        out_shape.append(jax.ShapeDtypeStruct((N_TOK, D_MODEL), dt))
        if kind == "gelu_ln":
            out_specs.append(pl.BlockSpec((TM, D_MODEL), lambda i: (jnp.maximum(i - NT_P, 0), 0)))
            out_shape.append(jax.ShapeDtypeStruct((N_SAMPLE, D_MODEL), _F32))
    return pl.pallas_call(
        functools.partial(_inproj_kernel, kinds),
        grid=(NT,),
        in_specs=[_tile_spec()] + _mod_specs(layer, 0) + _mod_specs(layer, 1) + [
            _row_spec(layer),
            pl.BlockSpec((None, D_MODEL, nsec * D_MODEL), lambda i: (layer, 0, sec0 // nsec)),
            pl.BlockSpec((None, SUBLANES, D_MODEL), lambda i: (layer, 0, 0))],
        out_specs=out_specs,
        out_shape=out_shape,
        compiler_params=_cparams(1),
        name=f"inproj_{sec0}",
    )(x, modp, mods, modp, mods, norm_w, w_in_bf, aux)


def _col_bcast(row):
    n = row.shape[-1]
    return jnp.broadcast_to(row, (n, n)).T


def _head_out(o, nw, sg):
    return (_rms(o, nw) * sg.astype(_F32)).astype(_BF)


def _hgrn_rows8(rows, q_ref, g_ref, i_ref, sg_ref, nw_ref, o_ref, s_get, s_put):
    t_row = lax.broadcasted_iota(_I32, (DEC_SEQ, HG_WIDTH), 0)
    t_col = lax.broadcasted_iota(_I32, (DEC_SEQ, 1), 0)
    g = g_ref[rows, :]
    b = g
    shift = 1
    while shift < DEC_SEQ:
        b = b + jnp.where(t_row >= shift, pltpu.roll(b, shift, axis=0), 0.0)
        shift *= 2
    kk = 1.0 - jnp.exp(g)
    q = q_ref[rows, :].astype(_F32)
    bend = b[DEC_SEQ - 1:DEC_SEQ, :]
    qe = (q * jnp.exp(b)).astype(_BF)
    kdec = (kk * jnp.exp(bend - b)).astype(_BF)
    e_end = jnp.exp(bend)
    iv_all = i_ref[rows, :]
    sg_all = sg_ref[rows, :]
    for h in range(HG_HEADS):
        hs = slice(h * HG_DK, (h + 1) * HG_DK)
        qh, bh, kh = q[:, hs], b[:, hs], kk[:, hs]
        iv = iv_all[:, hs]
        ivf = iv.astype(_F32)
        s_old = s_get(h)
        o = _dot(qe[:, hs], s_old.astype(_BF))
        for j in range(DEC_SEQ):
            decay = jnp.exp(jnp.minimum(bh - bh[j:j + 1, :], 0.0))
            w = jnp.sum(qh * decay * kh[j:j + 1, :], axis=-1, keepdims=True)
            o = o + jnp.where(t_col >= j, w, 0.0) * ivf[j:j + 1, :]
        upd = lax.dot_general(kdec[:, hs], iv, (((0,), (0,)), ((), ())),
                              preferred_element_type=_F32)
        s_put(h, _col_bcast(e_end[:, hs]) * s_old + upd)
        o_ref[rows, hs] = _head_out(o, nw_ref[...], sg_all[:, hs])


def _rows_of_chunks(x, row_in_chunk):
    parts = []
    for c in range(HG_BLK // HG_C):
        r = c * HG_C + row_in_chunk
        parts.append(jnp.broadcast_to(x[r:r + 1, :], (HG_C, x.shape[-1])))
    return jnp.concatenate(parts, axis=0)


def _hgrn_prompt_kernel(q_ref, g_ref, i_ref, sg_ref, nw_ref, o_ref, s_out_ref, s_scr):
    c = pl.program_id(1)

    @pl.when(c == 0)
    def _():
        s_scr[...] = jnp.zeros_like(s_scr)

    g = g_ref[...]
    row = lax.broadcasted_iota(_I32, (HG_BLK, HG_BLK), 0)
    col = lax.broadcasted_iota(_I32, (HG_BLK, HG_BLK), 1)
    tril = jnp.logical_and(row >= col, row // HG_C == col // HG_C).astype(_BF)
    g_hi = g.astype(_BF)
    r1 = g - g_hi.astype(_F32)
    g_mid = r1.astype(_BF)
    g_lo = (r1 - g_mid.astype(_F32)).astype(_BF)
    b = _dot(tril, g_hi) + _dot(tril, g_mid) + _dot(tril, g_lo)
    ref = _rows_of_chunks(b, HG_C // 2 - 1)
    in_range = jnp.max(jnp.abs(b - ref)) <= EXP_RANGE

    @pl.when(in_range)
    def _():
        kk = 1.0 - jnp.exp(g)
        q = q_ref[...].astype(_F32)
        bend = _rows_of_chunks(b, HG_C - 1)
        qt = (q * jnp.exp(b - ref)).astype(_BF)
        kt = (kk * jnp.exp(ref - b)).astype(_BF)
        qe = (q * jnp.exp(b)).astype(_BF)
        kdec = (kk * jnp.exp(bend - b)).astype(_BF)
        e_end = jnp.exp(bend)
        crow = lax.broadcasted_iota(_I32, (HG_C, HG_C), 0)
        ccol = lax.broadcasted_iota(_I32, (HG_C, HG_C), 1)
        causal = crow >= ccol
        for h in range(HG_HEADS):
            hs = slice(h * HG_DK, (h + 1) * HG_DK)
            s_cur = s_scr[h]
            for ci in range(HG_BLK // HG_C):
                rs = slice(ci * HG_C, (ci + 1) * HG_C)
                iv = i_ref[rs, hs]
                scores = lax.dot_general(qt[rs, hs], kt[rs, hs], (((1,), (1,)), ((), ())),
                                         preferred_element_type=_F32)
                scores = jnp.where(causal, scores, 0.0).astype(_BF)
                o = _dot(scores, iv) + _dot(qe[rs, hs], s_cur.astype(_BF))
                upd = lax.dot_general(kdec[rs, hs], iv, (((0,), (0,)), ((), ())),
                                      preferred_element_type=_F32)
                s_cur = _col_bcast(e_end[ci * HG_C:ci * HG_C + 1, hs]) * s_cur + upd
                o_ref[rs, hs] = _head_out(o, nw_ref[...], sg_ref[rs, hs])
            s_scr[h] = s_cur

    @pl.when(jnp.logical_not(in_range))
    def _():
        def s_put(h, val):
            s_scr[h] = val

        def sub_chunk(k, carry):
            rows = pl.ds(pl.multiple_of(k * DEC_SEQ, DEC_SEQ), DEC_SEQ)
            _hgrn_rows8(rows, q_ref, g_ref, i_ref, sg_ref, nw_ref, o_ref, lambda h: s_scr[h], s_put)
            return carry

        lax.fori_loop(0, HG_BLK // DEC_SEQ, sub_chunk, 0)

    @pl.when(c == pl.num_programs(1) - 1)
    def _():
        s_out_ref[...] = s_scr[...]


def _hgrn_prompt_call(layer, q, g, iv, sg, hg_norm_w):
    nc = SEQ // HG_BLK
    blk = pl.BlockSpec((HG_BLK, HG_WIDTH), lambda b, c: (b * nc + c, 0))
    return pl.pallas_call(
        _hgrn_prompt_kernel,
        grid=(BATCH, nc),
        in_specs=[blk, blk, blk, blk,
                  pl.BlockSpec((None, 1, HG_DV), lambda b, c: (layer, 0, 0))],
        out_specs=[blk,
                   pl.BlockSpec((None, HG_HEADS, HG_DK, HG_DV), lambda b, c: (b, 0, 0, 0))],
        out_shape=[jax.ShapeDtypeStruct((N_PROMPT, HG_WIDTH), _BF),
                   jax.ShapeDtypeStruct((BATCH, HG_HEADS, HG_DK, HG_DV), _F32)],
        scratch_shapes=[pltpu.VMEM((HG_HEADS, HG_DK, HG_DV), _F32)],
        compiler_params=_cparams(2),
        name="hgrn_prompt",
    )(q, g, iv, sg, hg_norm_w)


def _hgrn_sample_kernel(q_ref, g_ref, i_ref, sg_ref, nw_ref, s_in_ref, o_ref, s_out_ref):
    def seq_body(s, carry):
        rows = pl.ds(pl.multiple_of(s * DEC_SEQ, DEC_SEQ), DEC_SEQ)

        def s_put(h, val):
            s_out_ref[s, h] = val

        _hgrn_rows8(rows, q_ref, g_ref, i_ref, sg_ref, nw_ref, o_ref, lambda h: s_in_ref[s, h], s_put)
        return carry

    lax.fori_loop(0, HG_SB, seq_body, 0)


def _hgrn_sample_call(layer, q, g, iv, sg, hg_norm_w, state):
    rows = HG_SB * DEC_SEQ
    off = N_PROMPT // rows
    blk = pl.BlockSpec((rows, HG_WIDTH), lambda j: (off + j, 0))
    return pl.pallas_call(
        _hgrn_sample_kernel,
        grid=(DEC_BATCH // HG_SB,),
        in_specs=[blk, blk, blk, blk,
                  pl.BlockSpec((None, 1, HG_DV), lambda j: (layer, 0, 0)),
                  pl.BlockSpec((None, HG_SB, HG_HEADS, HG_DK, HG_DV), lambda j: (layer, j, 0, 0, 0))],
        out_specs=[pl.BlockSpec((rows, HG_WIDTH), lambda j: (j, 0)),
                   pl.BlockSpec((HG_SB, HG_HEADS, HG_DK, HG_DV), lambda j: (j, 0, 0, 0))],
        out_shape=[jax.ShapeDtypeStruct((N_SAMPLE, HG_WIDTH), _BF),
                   jax.ShapeDtypeStruct((DEC_BATCH, HG_HEADS, HG_DK, HG_DV), _F32)],
        compiler_params=_cparams(1),
        name="hgrn_sample",
    )(q, g, iv, sg, hg_norm_w, state)


def _route(h2, h2_bf, wr_hi_ref, wr_lo_ref, br_ref, route_ref, route_t_ref, cnt_ref):
    h2_lo = (h2 - h2_bf.astype(_F32)).astype(_BF)
    logits = (_dot(h2_bf, wr_hi_ref[...]) + (_dot(h2_lo, wr_hi_ref[...]) + _dot(h2_bf, wr_lo_ref[...]))
              + br_ref[...])
    lane = lax.broadcasted_iota(_I32, logits.shape, 1)
    neg = jnp.float32(-jnp.inf)
    logits = jnp.where(lane < N_EXPERTS, logits, neg)
    m1 = jnp.max(logits, axis=-1, keepdims=True)
    i1 = jnp.min(jnp.where(logits == m1, lane, LANES), axis=-1, keepdims=True)
    rest_l = jnp.where(lane == i1, neg, logits)
    m2 = jnp.max(rest_l, axis=-1, keepdims=True)
    i2 = jnp.min(jnp.where(rest_l == m2, lane, LANES), axis=-1, keepdims=True)
    e2 = jnp.exp(m2 - m1)
    w1 = 1.0 / (1.0 + e2)
    w2 = e2 / (1.0 + e2)
    hot1 = lane == i1
    hot2 = lane == i2
    tot = jnp.where(jnp.logical_or(hot1, hot2), 1.0, 0.0)
    trow = lax.broadcasted_iota(_I32, (TM, TM), 0)
    tcol = lax.broadcasted_iota(_I32, (TM, TM), 1)
    before = _dot((trow > tcol).astype(_BF), tot.astype(_BF))
    cnt = jnp.sum(tot, axis=0, keepdims=True)
    cnt_pad = jnp.ceil(cnt * (1.0 / RUN_ALIGN)) * RUN_ALIGN
    erow = lax.broadcasted_iota(_I32, (LANES, LANES), 0)
    ecol = lax.broadcasted_iota(_I32, (LANES, LANES), 1)
    start = _dot(jnp.broadcast_to(cnt_pad, (SUBLANES, LANES)).astype(_BF),
                 (erow < ecol).astype(_BF))[0:1, :]
    slot = before + start
    pos1 = jnp.sum(jnp.where(hot1, slot, 0.0), axis=-1, keepdims=True)
    pos2 = jnp.sum(jnp.where(hot2, slot, 0.0), axis=-1, keepdims=True)
    route = jnp.where(lane == 0, pos1, jnp.where(lane == 1, pos2,
                      jnp.where(lane == 2, w1, jnp.where(lane == 3, w2, 0.0))))
    route_ref[...] = route
    route_t_ref[...] = route.T[0:SUBLANES, :]
    cnt_ref[...] = jnp.broadcast_to(cnt, (SUBLANES, LANES)).astype(_I32)


def _mix_kernel(with_router, u_ref, v_ref, sga_ref, sgb_ref, op_ref, os_ref, x_ref,
                mp_g1, ms_g1, mp_sh, ms_sh, mp_sc, ms_sc,
                wmix_ref, bmix_ref, wa_ref, wb_ref, wo_ref, npost_ref, npre_ref, *rest):
    if with_router:
        wr_hi_ref, wr_lo_ref, br_ref, x1_ref, h2_ref, route_ref, route_t_ref, cnt_ref, a_scr = rest
    else:
        x1_ref, h2_ref, a_scr = rest
    for c in range(TM // GM_CHUNK):
        rows = slice(c * GM_CHUNK, (c + 1) * GM_CHUNK)
        for gi in range(GM_GROUPS):
            cols = slice(gi * GM_GROUP_DIM, (gi + 1) * GM_GROUP_DIM)
            mixed = _dot(wmix_ref[gi], v_ref[rows, cols])
            bias = bmix_ref[gi]
            mixed = mixed + jnp.concatenate([bias] * (GM_GROUP_DIM // LANES), axis=1)
            a_scr[rows, cols] = (u_ref[rows, cols].astype(_F32) * mixed).astype(_BF)
    br_a = _dot(a_scr[...], wa_ref[...])
    o = jnp.where(_is_sample_tile(), os_ref[...], op_ref[...])
    br_b = _dot(o, wb_ref[...])
    merged = sga_ref[...].astype(_F32) * br_a + sgb_ref[...].astype(_F32) * br_b
    mix = _dot(merged.astype(_BF), wo_ref[...])
    g1 = _mod_rows(mp_g1, ms_g1)
    sh = _mod_rows(mp_sh, ms_sh)
    sc = _mod_rows(mp_sc, ms_sc)
    nm = _rms(mix, npost_ref[...])
    x1 = x_ref[...] + _per_seq(nm, lambda t: t * g1[:, None, :])
    x1_ref[...] = x1
    xn = _rms(x1, npre_ref[...])
    h2 = _per_seq(xn, lambda t: t * (1.0 + sc)[:, None, :] + sh[:, None, :])
    h2_bf = h2.astype(_BF)
    h2_ref[...] = h2_bf
    if with_router:
        _route(h2, h2_bf, wr_hi_ref, wr_lo_ref, br_ref, route_ref, route_t_ref, cnt_ref)


def _mix_call(layer, moe_idx, u, v, sga, sgb, o_p, o_s, x, modp, mods, wmix, bmix,
              wa, wb, wo, npost, npre, wr_hi=None, wr_lo=None, br=None):
    with_router = moe_idx is not None
    ty = lambda i: (i >= NT_P).astype(_I32)
    wspec = pl.BlockSpec((None, D_MODEL, D_MODEL), lambda i: (layer, 0, 0))
    in_specs = [_tile_spec(), _tile_spec(), _tile_spec(), _tile_spec(),
                pl.BlockSpec((TM, HG_WIDTH), lambda i: (jnp.minimum(i, NT_P - 1), 0)),
                pl.BlockSpec((TM, HG_WIDTH), lambda i: (jnp.maximum(i - NT_P, 0), 0)),
                _tile_spec()]
    in_specs += _mod_specs(layer, 2) + _mod_specs(layer, 3) + _mod_specs(layer, 4)
    in_specs += [pl.BlockSpec((None, None, GM_GROUPS, GM_CHUNK, GM_CHUNK), lambda i: (layer, ty(i), 0, 0, 0)),
                 pl.BlockSpec((None, None, GM_GROUPS, GM_CHUNK, LANES), lambda i: (layer, ty(i), 0, 0, 0)),
                 wspec, wspec, wspec, _row_spec(layer), _row_spec(layer)]
    args = [u, v, sga, sgb, o_p, o_s, x, modp, mods, modp, mods, modp, mods,
            wmix, bmix, wa, wb, wo, npost, npre]
    out_specs = [_tile_spec(), _tile_spec()]
    out_shape = [jax.ShapeDtypeStruct((N_TOK, D_MODEL), _F32),
                 jax.ShapeDtypeStruct((N_TOK, D_MODEL), _BF)]
    if with_router:
        rspec = pl.BlockSpec((None, D_MODEL, LANES), lambda i: (moe_idx, 0, 0))
        in_specs += [rspec, rspec, pl.BlockSpec((None, 1, LANES), lambda i: (moe_idx, 0, 0))]
        args += [wr_hi, wr_lo, br]
        out_specs += [_tile_spec(LANES),
                      pl.BlockSpec((None, SUBLANES, TM), lambda i: (i, 0, 0)),
                      pl.BlockSpec((None, SUBLANES, LANES), lambda i: (i, 0, 0))]
        out_shape += [jax.ShapeDtypeStruct((N_TOK, LANES), _F32),
                      jax.ShapeDtypeStruct((NT, SUBLANES, TM), _F32),
                      jax.ShapeDtypeStruct((NT, SUBLANES, LANES), _I32)]
    return pl.pallas_call(
        functools.partial(_mix_kernel, with_router),
        grid=(NT,),
        in_specs=in_specs,
        out_specs=out_specs,
        out_shape=out_shape,
        scratch_shapes=[pltpu.VMEM((TM, GM_WIDTH), _BF)],
        compiler_params=_cparams(1),
        name="mix",
    )(*args)


def _ffn_partial(h, wg_ref, wu_ref, wd_ref):
    gate = _dot(h, wg_ref[...])
    up = _dot(h, wu_ref[...])
    act = (_silu(gate) * up).astype(_BF)
    return _dot(act, wd_ref[...])


def _residual_out(x1_ref, f, mp_g2, ms_g2, npost_ref, out_ref):
    g2 = _mod_rows(mp_g2, ms_g2)
    nf = _rms(f, npost_ref[...])
    out_ref[...] = x1_ref[...] + _per_seq(nf, lambda t: t * g2[:, None, :])


def _ffn_kernel(h_ref, x1_ref, mp_g2, ms_g2, wg_ref, wu_ref, wd_ref, npost_ref, out_ref, acc_ref):
    k = pl.program_id(1)
    part = _ffn_partial(h_ref[...], wg_ref, wu_ref, wd_ref)

    @pl.when(k == 0)
    def _():
        acc_ref[...] = part

    @pl.when(k > 0)
    def _():
        acc_ref[...] += part

    @pl.when(k == pl.num_programs(1) - 1)
    def _():
        _residual_out(x1_ref, acc_ref[...], mp_g2, ms_g2, npost_ref, out_ref)


def _ffn_call(layer, j, h2, x1, modp, mods, wg, wu, wd, npost):
    nk = D_FF // FF_CHUNK
    tile = pl.BlockSpec((TM, D_MODEL), lambda i, k: (i, 0))
    return pl.pallas_call(
        _ffn_kernel,
        grid=(NT, nk),
        in_specs=[tile, tile] + _mod_specs(layer, 5) + [
            pl.BlockSpec((None, D_MODEL, FF_CHUNK), lambda i, k: (j, 0, k)),
            pl.BlockSpec((None, D_MODEL, FF_CHUNK), lambda i, k: (j, 0, k)),
            pl.BlockSpec((None, FF_CHUNK, D_MODEL), lambda i, k: (j, k, 0)),
            pl.BlockSpec((None, 1, D_MODEL), lambda i, k: (layer, 0, 0))],
        out_specs=tile,
        out_shape=jax.ShapeDtypeStruct((N_TOK, D_MODEL), _F32),
        scratch_shapes=[pltpu.VMEM((TM, D_MODEL), _F32)],
        compiler_params=_cparams(2),
        name="ffn",
    )(h2, x1, modp, mods, wg, wu, wd, npost)


def _route_tables(cnt):
    cnt = cnt[:, 0, :N_EXPERTS]
    n = (cnt + (RUN_ALIGN - 1)) // RUN_ALIGN * RUN_ALIGN
    s = jnp.cumsum(n, axis=1) - n
    rows_e = jnp.sum(n, axis=0)
    region = (rows_e + (TG - 1)) // TG * TG
    region_end = jnp.cumsum(region)
    off = region_end - region
    p = off[None, :] + jnp.cumsum(n, axis=0) - n
    n_tiles = region_end[-1] // TG
    tile_row0 = jnp.minimum(jnp.arange(R_TILES, dtype=_I32), n_tiles - 1) * TG
    tile_expert = jnp.sum((tile_row0[:, None] >= region_end[None, :]).astype(_I32), axis=1)
    flat = lambda a: a.reshape(-1).astype(_I32)
    return dict(p=flat(p), s=flat(s), n=flat(n), tail_p=flat(off + rows_e), tail_n=flat(region - rows_e),
                tile_expert=flat(tile_expert), n_tiles=flat(n_tiles))


def _for_each_run_piece(n, src0, dst0, fn):
    for size in RUN_SIZES:
        done = n & (-2 * size)

        @pl.when((n & size) != 0)
        def _():
            fn(pl.multiple_of(src0 + done, RUN_ALIGN), pl.multiple_of(dst0 + done, RUN_ALIGN), size)


def _dispatch_kernel(p_tab, s_tab, n_tab, tail_p, tail_n, nt_ref, h_ref, rt_ref, xs_ref,
                     comp_scr, zero_scr, sem):
    t = pl.program_id(0)
    pos1 = rt_ref[0:1, :].astype(_I32)
    pos2 = rt_ref[1:2, :].astype(_I32)
    r = lax.broadcasted_iota(_I32, (CB, TM), 0)
    perm = jnp.where(jnp.logical_or(r == pos1, r == pos2), 1.0, 0.0).astype(_BF)
    comp_scr[...] = _dot(perm, h_ref[...]).astype(_BF)

    def runs(go):
        for e in range(N_EXPERTS):
            idx = t * N_EXPERTS + e

            def piece(src, dst, size):
                go(pltpu.make_async_copy(comp_scr.at[pl.ds(src, size)], xs_ref.at[pl.ds(dst, size)], sem))

            _for_each_run_piece(n_tab[idx], s_tab[idx], p_tab[idx], piece)

    runs(lambda cp: cp.start())
    runs(lambda cp: cp.wait())

    @pl.when(t == NT - 1)
    def _():
        zero_scr[...] = jnp.zeros_like(zero_scr)

        def tails(go):
            for e in range(N_EXPERTS):
                def piece(src, dst, size):
                    go(pltpu.make_async_copy(zero_scr.at[pl.ds(0, size)], xs_ref.at[pl.ds(dst, size)], sem))

                _for_each_run_piece(tail_n[e], 0, tail_p[e], piece)

        tails(lambda cp: cp.start())
        tails(lambda cp: cp.wait())

        def spare_tile(go):
            def body(r, carry):
                dst = pl.multiple_of(r * TG, TG)
                go(pltpu.make_async_copy(zero_scr, xs_ref.at[pl.ds(dst, TG)], sem))
                return carry
            lax.fori_loop(nt_ref[0], R_TILES, body, 0)

        spare_tile(lambda cp: cp.start())
        spare_tile(lambda cp: cp.wait())


def _dispatch_call(tabs, h2, route_t):
    grid_spec = pltpu.PrefetchScalarGridSpec(
        num_scalar_prefetch=6,
        grid=(NT,),
        in_specs=[_tile_spec(), pl.BlockSpec((None, SUBLANES, TM), lambda i, *_: (i, 0, 0))],
        out_specs=pl.BlockSpec(memory_space=pl.ANY),
        scratch_shapes=[pltpu.VMEM((CB, D_MODEL), _BF), pltpu.VMEM((TG, D_MODEL), _BF),
                        pltpu.SemaphoreType.DMA(())])
    return pl.pallas_call(
        _dispatch_kernel,
        grid_spec=grid_spec,
        out_shape=jax.ShapeDtypeStruct((N_ROWS, D_MODEL), _BF),
        compiler_params=_cparams(1),
        name="dispatch",
    )(tabs["p"], tabs["s"], tabs["n"], tabs["tail_p"], tabs["tail_n"], tabs["n_tiles"], h2, route_t)


def _expert_kernel(te_ref, nt_ref, x_ref, wg_ref, wu_ref, wd_ref, y_ref, acc_ref):
    r = pl.program_id(0)
    k = pl.program_id(1)

    @pl.when(r < nt_ref[0])
    def _():
        part = _ffn_partial(x_ref[...], wg_ref, wu_ref, wd_ref)

        @pl.when(k == 0)
        def _():
            acc_ref[...] = part

        @pl.when(k > 0)
        def _():
            acc_ref[...] += part

        @pl.when(k == pl.num_programs(1) - 1)
        def _():
            y_ref[...] = acc_ref[...].astype(_BF)

    @pl.when(r >= nt_ref[0])
    def _():
        y_ref[...] = jnp.zeros_like(y_ref)


def _expert_call(j, tabs, xs, wg, wu, wd):
    nk = D_FF // FF_CHUNK
    row_blk = lambda r, k, te, nt: (jnp.minimum(r, nt[0] - 1), 0)
    kk = lambda r, k, nt: jnp.where(r < nt[0], k, nk - 1)
    grid_spec = pltpu.PrefetchScalarGridSpec(
        num_scalar_prefetch=2,
        grid=(R_TILES, nk),
        in_specs=[pl.BlockSpec((TG, D_MODEL), row_blk),
                  pl.BlockSpec((None, None, D_MODEL, FF_CHUNK), lambda r, k, te, nt: (j, te[r], 0, kk(r, k, nt))),
                  pl.BlockSpec((None, None, D_MODEL, FF_CHUNK), lambda r, k, te, nt: (j, te[r], 0, kk(r, k, nt))),
                  pl.BlockSpec((None, None, FF_CHUNK, D_MODEL), lambda r, k, te, nt: (j, te[r], kk(r, k, nt), 0))],
        out_specs=pl.BlockSpec((TG, D_MODEL), lambda r, k, te, nt: (r, 0)),
        scratch_shapes=[pltpu.VMEM((TG, D_MODEL), _F32)])
    return pl.pallas_call(
        _expert_kernel,
        grid_spec=grid_spec,
        out_shape=jax.ShapeDtypeStruct((N_ROWS, D_MODEL), _BF),
        compiler_params=_cparams(2),
        name="experts",
    )(tabs["tile_expert"], tabs["n_tiles"], xs, wg, wu, wd)


def _combine_kernel(p_tab, s_tab, n_tab, route_ref, x1_ref, mp_g2, ms_g2, npost_ref, ys_ref,
                    out_ref, yc_scr, sem):
    t = pl.program_id(0)
    yc_scr[...] = jnp.zeros_like(yc_scr)

    def runs(go):
        for e in range(N_EXPERTS):
            idx = t * N_EXPERTS + e

            def piece(src, dst, size):
                go(pltpu.make_async_copy(ys_ref.at[pl.ds(dst, size)], yc_scr.at[pl.ds(src, size)], sem))

            _for_each_run_piece(n_tab[idx], s_tab[idx], p_tab[idx], piece)

    runs(lambda cp: cp.start())
    runs(lambda cp: cp.wait())
    route = route_ref[...]
    pos1 = route[:, 0:1].astype(_I32)
    pos2 = route[:, 1:2].astype(_I32)
    w1 = route[:, 2:3]
    w2 = route[:, 3:4]
    slot = lax.broadcasted_iota(_I32, (TM, CB), 1)
    yc = yc_scr[...]
    pick1 = jnp.where(slot == pos1, 1.0, 0.0).astype(_BF)
    pick2 = jnp.where(slot == pos2, 1.0, 0.0).astype(_BF)
    f = w1 * _dot(pick1, yc) + w2 * _dot(pick2, yc)
    _residual_out(x1_ref, f, mp_g2, ms_g2, npost_ref, out_ref)


def _combine_call(layer, tabs, route, x1, modp, mods, npost, ys):
    grid_spec = pltpu.PrefetchScalarGridSpec(
        num_scalar_prefetch=3,
        grid=(NT,),
        in_specs=[_tile_spec(LANES), _tile_spec()] + _mod_specs(layer, 5) + [
            _row_spec(layer), pl.BlockSpec(memory_space=pl.ANY)],
        out_specs=_tile_spec(),
        scratch_shapes=[pltpu.VMEM((CB, D_MODEL), _BF), pltpu.SemaphoreType.DMA(())])
    return pl.pallas_call(
        _combine_kernel,
        grid_spec=grid_spec,
        out_shape=jax.ShapeDtypeStruct((N_TOK, D_MODEL), _F32),
        compiler_params=_cparams(1),
        name="combine",
    )(tabs["p"], tabs["s"], tabs["n"], route, x1, modp, mods, npost, ys)


def _spatial_tables(gm_ws, gm_bs):
    mask_p = jnp.tril(jnp.ones((GM_CHUNK, GM_CHUNK), bool))
    w_p = jnp.where(mask_p, gm_ws, 0.0)
    mask_s = jnp.tril(jnp.ones((DEC_SEQ, DEC_SEQ), bool))
    w_small = jnp.where(mask_s, gm_ws[:, :, :DEC_SEQ, :DEC_SEQ], 0.0)
    eye = jnp.eye(GM_CHUNK // DEC_SEQ, dtype=gm_ws.dtype)
    w_s = jnp.einsum("ab,lgts->lgatbs", eye, w_small).reshape(gm_ws.shape)
    b_p = gm_bs
    b_s = jnp.tile(gm_bs[:, :, :DEC_SEQ], (1, 1, GM_CHUNK // DEC_SEQ))
    wmix = jnp.stack([w_p, w_s], axis=1).astype(_BF)
    bmix = jnp.stack([b_p, b_s], axis=1)[..., None]
    bmix = jnp.broadcast_to(bmix, bmix.shape[:-1] + (LANES,)).astype(_F32)
    return wmix, bmix


def kernel(x_prompt, x_sample, c_prompt, c_sample, state_hgrn, w_in, gm_ln_w, gm_ln_b, gm_ws, gm_bs,
           hg_lb_raw, hg_norm_w, w_branch_a, w_branch_b, w_out, w_ada, b_ada,
           norm_pre_mix, norm_post_mix, norm_pre_ffn, norm_post_ffn,
           w_ffn_gate, w_ffn_up, w_ffn_down, w_router, b_router, w_exp_gate, w_exp_up, w_exp_down):
    x = jnp.concatenate([x_prompt.reshape(N_PROMPT, D_MODEL), x_sample.reshape(N_SAMPLE, D_MODEL)], axis=0)
    c_all = jnp.concatenate([c_prompt, c_sample], axis=0)
    mod = _ada_call(c_all, w_ada, b_ada)
    modp = mod[:, :BATCH].reshape(DEPTH, BATCH, 1, 6 * D_MODEL)
    mods = mod[:, BATCH:]
    loglb = _lb_call(hg_lb_raw)
    aux = jnp.zeros((DEPTH, SUBLANES, D_MODEL), _F32)
    aux = aux.at[:, 0].set(gm_ln_w).at[:, 1].set(gm_ln_b).at[:, 2].set(loglb)
    wmix, bmix = _spatial_tables(gm_ws, gm_bs)
    row3 = lambda a: a.reshape(a.shape[0], 1, a.shape[1])
    w_in_bf = w_in.astype(_BF)
    wa_bf, wb_bf, wo_bf = w_branch_a.astype(_BF), w_branch_b.astype(_BF), w_out.astype(_BF)
    wfg, wfu, wfd = w_ffn_gate.astype(_BF), w_ffn_up.astype(_BF), w_ffn_down.astype(_BF)
    weg, weu, wed = w_exp_gate.astype(_BF), w_exp_up.astype(_BF), w_exp_down.astype(_BF)
    wr_pad = jnp.pad(w_router, ((0, 0), (0, 0), (0, LANES - N_EXPERTS)))
    wr_hi = wr_pad.astype(_BF)
    wr_lo = (wr_pad - wr_hi.astype(_F32)).astype(_BF)
    br_pad = jnp.pad(b_router, ((0, 0), (0, LANES - N_EXPERTS))).reshape(-1, 1, LANES)
    npre_mix, npost_mix = row3(norm_pre_mix), row3(norm_post_mix)
    npre_ffn, npost_ffn = row3(norm_pre_ffn), row3(norm_post_ffn)
    hg_nw = row3(hg_norm_w)

    hp, hs, vs = [], [], []
    for l in range(DEPTH):
        u, v, v32, q, g = _inproj_call(l, 0, ("gelu", "gelu_ln", "silu", "logf"),
                                       x, modp, mods, npre_mix, w_in_bf, aux)
        iv, sg, sga, sgb = _inproj_call(l, 4, ("id", "silu", "sigmoid", "sigmoid"),
                                        x, modp, mods, npre_mix, w_in_bf, aux)
        o_p, s_p = _hgrn_prompt_call(l, q, g, iv, sg, hg_nw)
        o_s, s_s = _hgrn_sample_call(l, q, g, iv, sg, hg_nw, state_hgrn)
        j = l // 2
        if l % 2 == 0:
            x1, h2 = _mix_call(l, None, u, v, sga, sgb, o_p, o_s, x, modp, mods, wmix, bmix,
                               wa_bf, wb_bf, wo_bf, npost_mix, npre_ffn)
            x = _ffn_call(l, j, h2, x1, modp, mods, wfg, wfu, wfd, npost_ffn)
        else:
            x1, h2, route, route_t, cnt = _mix_call(l, j, u, v, sga, sgb, o_p, o_s, x, modp, mods, wmix, bmix,
                                                    wa_bf, wb_bf, wo_bf, npost_mix, npre_ffn,
                                                    wr_hi, wr_lo, br_pad)
            tabs = _route_tables(cnt)
            xs = _dispatch_call(tabs, h2, route_t)
            ys = _expert_call(j, tabs, xs, weg, weu, wed)
            x = _combine_call(l, tabs, route, x1, modp, mods, npost_ffn, ys)
        hp.append(s_p)
        hs.append(s_s)
        vs.append(v32.reshape(DEC_BATCH, DEC_SEQ, GM_WIDTH))
    y_prompt = x[:N_PROMPT].reshape(BATCH, SEQ, D_MODEL)
    y_sample = x[N_PROMPT:].reshape(DEC_BATCH, DEC_SEQ, D_MODEL)
    return (y_prompt, y_sample, jnp.stack(hp, axis=0), jnp.stack(hs, axis=0), jnp.stack(vs, axis=0))
```

```python
import functools
import math

import jax
import jax.numpy as jnp
from jax import lax
from jax.experimental import pallas as pl
from jax.experimental.pallas import tpu as pltpu

D_MODEL = 1024
BATCH = 8
SEQ = 2048
DEPTH = 4
DEC_BATCH = 128
DEC_SEQ = 8
GM_WIDTH = 1024
GM_GROUPS = 4
GM_GROUP_DIM = GM_WIDTH // GM_GROUPS
GM_CHUNK = 128
HG_HEADS = 8
HG_DK = 128
HG_DV = 128
HG_WIDTH = HG_HEADS * HG_DK
D_FF = 2816
N_EXPERTS = 8
TOP_K = 2
IN_COLS = 2 * GM_WIDTH + 4 * HG_WIDTH + 2 * D_MODEL
EPS = 1e-6

LANES = 128
SUBLANES = 8
BF16_ROWS = 16
TM = 512
N_PROMPT = BATCH * SEQ
N_SAMPLE = DEC_BATCH * DEC_SEQ
N_TOK = N_PROMPT + N_SAMPLE
NT_P = N_PROMPT // TM
NT_S = N_SAMPLE // TM
NT = NT_P + NT_S
TILES_PER_SEQ = SEQ // TM
SEQ_PER_TILE = TM // DEC_SEQ
HG_C = 64
HG_BLK = 256
HG_SB = 8
FF_CHUNK = D_FF // 2
EXP_RANGE = 70.0
VMEM_LIMIT = 56 * 1024 * 1024

TG = 512
RUN_ALIGN = BF16_ROWS
RUN_SIZES = tuple(TM >> k for k in range((TM // RUN_ALIGN).bit_length()))
CB = TOP_K * TM + N_EXPERTS * RUN_ALIGN
_ROWS_BOUND = TOP_K * N_TOK + NT * N_EXPERTS * (RUN_ALIGN - 1) + N_EXPERTS * (TG - RUN_ALIGN)
R_TILES = -(-_ROWS_BOUND // TG)
N_ROWS = R_TILES * TG

_BF = jnp.bfloat16
_F32 = jnp.float32
_I32 = jnp.int32


def _cparams(n_axes):
    return pltpu.CompilerParams(dimension_semantics=("arbitrary",) * n_axes,
                                vmem_limit_bytes=VMEM_LIMIT)


def _dot(a, b):
    return jnp.dot(a, b, preferred_element_type=_F32)


def _silu(x):
    return x * (1.0 / (1.0 + jnp.exp(-x)))


def _sigmoid(x):
    return 1.0 / (1.0 + jnp.exp(-x))


def _gelu(x):
    return 0.5 * x * (1.0 + lax.erf(x * (1.0 / math.sqrt(2.0))))


def _rms(x, w):
    return x * lax.rsqrt(jnp.mean(x * x, axis=-1, keepdims=True) + EPS) * w


def _is_sample_tile():
    return pl.program_id(0) >= NT_P


def _mod_rows(mp_ref, ms_ref):
    mp = jnp.broadcast_to(mp_ref[...], ms_ref.shape)
    return jnp.where(_is_sample_tile(), ms_ref[...], mp)


def _per_seq(x, fn_rows):
    x3 = x.reshape(SEQ_PER_TILE, DEC_SEQ, x.shape[-1])
    return fn_rows(x3).reshape(TM, x.shape[-1])


def _mod_specs(layer, col_block):
    mp = pl.BlockSpec((None, None, 1, D_MODEL),
                      lambda i, *_: (layer, jnp.minimum(i // TILES_PER_SEQ, BATCH - 1), 0, col_block))
    ms = pl.BlockSpec((None, SEQ_PER_TILE, D_MODEL),
                      lambda i, *_: (layer, jnp.maximum(i - NT_P, 0), col_block))
    return [mp, ms]


def _tile_spec(width=D_MODEL):
    return pl.BlockSpec((TM, width), lambda i, *_: (i, 0))


def _row_spec(layer, width=D_MODEL):
    return pl.BlockSpec((None, 1, width), lambda i, *_: (layer, 0, 0))


def _ada_kernel(c_ref, w_ref, b_ref, o_ref):
    s = _silu(c_ref[...]).astype(_BF)
    o_ref[...] = _dot(s, w_ref[...].astype(_BF)) + b_ref[...]


def _ada_call(c_all, w_ada, b_ada):
    nb = 4
    wcol = 6 * D_MODEL // nb
    rows = c_all.shape[0]
    return pl.pallas_call(
        _ada_kernel,
        grid=(DEPTH, nb),
        in_specs=[pl.BlockSpec((rows, D_MODEL), lambda l, j: (0, 0)),
                  pl.BlockSpec((None, D_MODEL, wcol), lambda l, j: (l, 0, j)),
                  pl.BlockSpec((None, 1, wcol), lambda l, j: (l, 0, j))],
        out_specs=pl.BlockSpec((None, rows, wcol), lambda l, j: (l, 0, j)),
        out_shape=jax.ShapeDtypeStruct((DEPTH, rows, 6 * D_MODEL), _F32),
        compiler_params=_cparams(2),
        name="ada",
    )(c_all, w_ada, b_ada.reshape(DEPTH, 1, 6 * D_MODEL))


def _lb_kernel(raw_ref, loglb_ref):
    raw = raw_ref[...]
    m = jnp.max(raw, axis=0, keepdims=True)
    e = jnp.exp(raw - m)
    p = e / jnp.sum(e, axis=0, keepdims=True)
    acc = jnp.zeros((1, HG_WIDTH), _F32)
    for l in range(DEPTH):
        if l > 0:
            acc = acc + p[l:l + 1, :]
        loglb_ref[l:l + 1, :] = jnp.log(acc)


def _lb_call(raw):
    return pl.pallas_call(
        _lb_kernel,
        out_shape=jax.ShapeDtypeStruct((DEPTH, HG_WIDTH), _F32),
        name="lower_bounds",
    )(raw)


def _modulated_norm(x, nw_ref, mp_sh, ms_sh, mp_sc, ms_sc):
    xn = _rms(x, nw_ref[...])
    sh = _mod_rows(mp_sh, ms_sh)
    sc = _mod_rows(mp_sc, ms_sc)
    return _per_seq(xn, lambda x3: x3 * (1.0 + sc)[:, None, :] + sh[:, None, :])


def _prenorm_kernel(xp_ref, xs_ref, mp_sh, ms_sh, mp_sc, ms_sc, nw_ref, h_ref):
    x = jnp.where(_is_sample_tile(), xs_ref[...], xp_ref[...])
    h_ref[...] = _modulated_norm(x, nw_ref, mp_sh, ms_sh, mp_sc, ms_sc).astype(_BF)


def _split_specs(width=D_MODEL):
    return [pl.BlockSpec((TM, width), lambda i, *_: (jnp.minimum(i, NT_P - 1), 0)),
            pl.BlockSpec((TM, width), lambda i, *_: (jnp.maximum(i - NT_P, 0), 0))]


def _prenorm_call(layer, x_p, x_s, modp, mods, norm_w):
    return pl.pallas_call(
        _prenorm_kernel,
        grid=(NT,),
        in_specs=_split_specs() + _mod_specs(layer, 0) + _mod_specs(layer, 1) + [_row_spec(layer)],
        out_specs=_tile_spec(),
        out_shape=jax.ShapeDtypeStruct((N_TOK, D_MODEL), _BF),
        compiler_params=_cparams(1),
        name="prenorm",
    )(x_p, x_s, modp, mods, modp, mods, norm_w)


def _inproj_kernel(kinds, h_ref, w_ref, aux_ref, *out_refs):
    h = h_ref[...]
    outs = list(out_refs)
    for j, kind in enumerate(kinds):
        z = _dot(h, w_ref[:, j * D_MODEL:(j + 1) * D_MODEL])
        if kind == "gelu":
            outs.pop(0)[...] = _gelu(z).astype(_BF)
        elif kind == "gelu_ln":
            a = _gelu(z)
            mu = jnp.mean(a, axis=-1, keepdims=True)
            ac = a - mu
            vn = ac * lax.rsqrt(jnp.mean(ac * ac, axis=-1, keepdims=True) + EPS)
            vn = vn * aux_ref[0:1, :] + aux_ref[1:2, :]
            outs.pop(0)[...] = vn.astype(_BF)
            v32_ref = outs.pop(0)

            @pl.when(_is_sample_tile())
            def _():
                v32_ref[...] = vn
        elif kind == "silu":
            outs.pop(0)[...] = _silu(z).astype(_BF)
        elif kind == "sigmoid":
            outs.pop(0)[...] = _sigmoid(z).astype(_BF)
        elif kind == "id":
            outs.pop(0)[...] = z.astype(_BF)
        elif kind == "logf":
            loglb = aux_ref[2:3, :]
            tail = jnp.log(1.0 + jnp.exp(-jnp.abs(z)))
            a = jnp.minimum(z, 0.0) - tail
            c = loglb - jnp.maximum(z, 0.0) - tail
            outs.pop(0)[...] = jnp.maximum(a, c) + jnp.log(1.0 + jnp.exp(-jnp.abs(z - loglb)))
        else:
            raise ValueError(kind)


def _inproj_call(layer, sec0, kinds, h, w_in_bf, aux):
    nsec = len(kinds)
    out_specs, out_shape = [], []
    for kind in kinds:
        dt = _F32 if kind == "logf" else _BF
        out_specs.append(_tile_spec())
        out_shape.append(jax.ShapeDtypeStruct((N_TOK, D_MODEL), dt))
        if kind == "gelu_ln":
            out_specs.append(pl.BlockSpec((TM, D_MODEL), lambda i: (jnp.maximum(i - NT_P, 0), 0)))
            out_shape.append(jax.ShapeDtypeStruct((N_SAMPLE, D_MODEL), _F32))
    return pl.pallas_call(
        functools.partial(_inproj_kernel, kinds),
        grid=(NT,),
        in_specs=[_tile_spec(),
                  pl.BlockSpec((None, D_MODEL, nsec * D_MODEL), lambda i: (layer, 0, sec0 // nsec)),
                  pl.BlockSpec((None, SUBLANES, D_MODEL), lambda i: (layer, 0, 0))],
        out_specs=out_specs,
        out_shape=out_shape,
        compiler_params=_cparams(1),
        name=f"inproj_{sec0}",
    )(h, w_in_bf, aux)


def _col_bcast(row):
    n = row.shape[-1]
    return jnp.broadcast_to(row, (n, n)).T


def _head_out(o, nw, sg):
    return (_rms(o, nw) * sg.astype(_F32)).astype(_BF)


def _hgrn_rows8(rows, q_ref, g_ref, i_ref, sg_ref, nw_ref, o_ref, s_get, s_put):
    t_row = lax.broadcasted_iota(_I32, (DEC_SEQ, HG_WIDTH), 0)
    t_col = lax.broadcasted_iota(_I32, (DEC_SEQ, 1), 0)
    g = g_ref[rows, :]
    b = g
    shift = 1
    while shift < DEC_SEQ:
        b = b + jnp.where(t_row >= shift, pltpu.roll(b, shift, axis=0), 0.0)
        shift *= 2
    kk = 1.0 - jnp.exp(g)
    q = q_ref[rows, :].astype(_F32)
    bend = b[DEC_SEQ - 1:DEC_SEQ, :]
    qe = (q * jnp.exp(b)).astype(_BF)
    kdec = (kk * jnp.exp(bend - b)).astype(_BF)
    e_end = jnp.exp(bend)
    iv_all = i_ref[rows, :]
    sg_all = sg_ref[rows, :]
    for h in range(HG_HEADS):
        hs = slice(h * HG_DK, (h + 1) * HG_DK)
        qh, bh, kh = q[:, hs], b[:, hs], kk[:, hs]
        iv = iv_all[:, hs]
        ivf = iv.astype(_F32)
        s_old = s_get(h)
        o = _dot(qe[:, hs], s_old.astype(_BF))
        for j in range(DEC_SEQ):
            decay = jnp.exp(jnp.minimum(bh - bh[j:j + 1, :], 0.0))
            w = jnp.sum(qh * decay * kh[j:j + 1, :], axis=-1, keepdims=True)
            o = o + jnp.where(t_col >= j, w, 0.0) * ivf[j:j + 1, :]
        upd = lax.dot_general(kdec[:, hs], iv, (((0,), (0,)), ((), ())),
                              preferred_element_type=_F32)
        s_put(h, _col_bcast(e_end[:, hs]) * s_old + upd)
        o_ref[rows, hs] = _head_out(o, nw_ref[...], sg_all[:, hs])


def _rows_of_chunks(x, row_in_chunk):
    parts = []
    for c in range(HG_BLK // HG_C):
        r = c * HG_C + row_in_chunk
        parts.append(jnp.broadcast_to(x[r:r + 1, :], (HG_C, x.shape[-1])))
    return jnp.concatenate(parts, axis=0)


def _hgrn_prompt_kernel(q_ref, g_ref, i_ref, sg_ref, nw_ref, o_ref, s_out_ref, s_scr):
    c = pl.program_id(1)

    @pl.when(c == 0)
    def _():
        s_scr[...] = jnp.zeros_like(s_scr)

    g = g_ref[...]
    row = lax.broadcasted_iota(_I32, (HG_BLK, HG_BLK), 0)
    col = lax.broadcasted_iota(_I32, (HG_BLK, HG_BLK), 1)
    tril = jnp.logical_and(row >= col, row // HG_C == col // HG_C).astype(_BF)
    g_hi = g.astype(_BF)
    r1 = g - g_hi.astype(_F32)
    g_mid = r1.astype(_BF)
    g_lo = (r1 - g_mid.astype(_F32)).astype(_BF)
    b = _dot(tril, g_hi) + _dot(tril, g_mid) + _dot(tril, g_lo)
    ref = _rows_of_chunks(b, HG_C // 2 - 1)
    in_range = jnp.max(jnp.abs(b - ref)) <= EXP_RANGE

    @pl.when(in_range)
    def _():
        kk = 1.0 - jnp.exp(g)
        q = q_ref[...].astype(_F32)
        bend = _rows_of_chunks(b, HG_C - 1)
        qt = (q * jnp.exp(b - ref)).astype(_BF)
        kt = (kk * jnp.exp(ref - b)).astype(_BF)
        qe = (q * jnp.exp(b)).astype(_BF)
        kdec = (kk * jnp.exp(bend - b)).astype(_BF)
        e_end = jnp.exp(bend)
        crow = lax.broadcasted_iota(_I32, (HG_C, HG_C), 0)
        ccol = lax.broadcasted_iota(_I32, (HG_C, HG_C), 1)
        causal = crow >= ccol
        for h in range(HG_HEADS):
            hs = slice(h * HG_DK, (h + 1) * HG_DK)
            s_cur = s_scr[h]
            for ci in range(HG_BLK // HG_C):
                rs = slice(ci * HG_C, (ci + 1) * HG_C)
                iv = i_ref[rs, hs]
                scores = lax.dot_general(qt[rs, hs], kt[rs, hs], (((1,), (1,)), ((), ())),
                                         preferred_element_type=_F32)
                scores = jnp.where(causal, scores, 0.0).astype(_BF)
                o = _dot(scores, iv) + _dot(qe[rs, hs], s_cur.astype(_BF))
                upd = lax.dot_general(kdec[rs, hs], iv, (((0,), (0,)), ((), ())),
                                      preferred_element_type=_F32)
                s_cur = _col_bcast(e_end[ci * HG_C:ci * HG_C + 1, hs]) * s_cur + upd
                o_ref[rs, hs] = _head_out(o, nw_ref[...], sg_ref[rs, hs])
            s_scr[h] = s_cur

    @pl.when(jnp.logical_not(in_range))
    def _():
        def s_put(h, val):
            s_scr[h] = val

        def sub_chunk(k, carry):
            rows = pl.ds(pl.multiple_of(k * DEC_SEQ, DEC_SEQ), DEC_SEQ)
            _hgrn_rows8(rows, q_ref, g_ref, i_ref, sg_ref, nw_ref, o_ref, lambda h: s_scr[h], s_put)
            return carry

        lax.fori_loop(0, HG_BLK // DEC_SEQ, sub_chunk, 0)

    @pl.when(c == pl.num_programs(1) - 1)
    def _():
        s_out_ref[...] = s_scr[...]


def _hgrn_prompt_call(layer, q, g, iv, sg, hg_norm_w):
    nc = SEQ // HG_BLK
    blk = pl.BlockSpec((HG_BLK, HG_WIDTH), lambda b, c: (b * nc + c, 0))
    return pl.pallas_call(
        _hgrn_prompt_kernel,
        grid=(BATCH, nc),
        in_specs=[blk, blk, blk, blk,
                  pl.BlockSpec((None, 1, HG_DV), lambda b, c: (layer, 0, 0))],
        out_specs=[blk,
                   pl.BlockSpec((None, HG_HEADS, HG_DK, HG_DV), lambda b, c: (b, 0, 0, 0))],
        out_shape=[jax.ShapeDtypeStruct((N_PROMPT, HG_WIDTH), _BF),
                   jax.ShapeDtypeStruct((BATCH, HG_HEADS, HG_DK, HG_DV), _F32)],
        scratch_shapes=[pltpu.VMEM((HG_HEADS, HG_DK, HG_DV), _F32)],
        compiler_params=_cparams(2),
        name="hgrn_prompt",
    )(q, g, iv, sg, hg_norm_w)


def _hgrn_sample_kernel(q_ref, g_ref, i_ref, sg_ref, nw_ref, s_in_ref, o_ref, s_out_ref):
    def seq_body(s, carry):
        rows = pl.ds(pl.multiple_of(s * DEC_SEQ, DEC_SEQ), DEC_SEQ)

        def s_put(h, val):
            s_out_ref[s, h] = val

        _hgrn_rows8(rows, q_ref, g_ref, i_ref, sg_ref, nw_ref, o_ref, lambda h: s_in_ref[s, h], s_put)
        return carry

    lax.fori_loop(0, HG_SB, seq_body, 0, unroll=2)


def _hgrn_sample_call(layer, q, g, iv, sg, hg_norm_w, state):
    rows = HG_SB * DEC_SEQ
    off = N_PROMPT // rows
    blk = pl.BlockSpec((rows, HG_WIDTH), lambda j: (off + j, 0))
    return pl.pallas_call(
        _hgrn_sample_kernel,
        grid=(DEC_BATCH // HG_SB,),
        in_specs=[blk, blk, blk, blk,
                  pl.BlockSpec((None, 1, HG_DV), lambda j: (layer, 0, 0)),
                  pl.BlockSpec((None, HG_SB, HG_HEADS, HG_DK, HG_DV), lambda j: (layer, j, 0, 0, 0))],
        out_specs=[pl.BlockSpec((rows, HG_WIDTH), lambda j: (j, 0)),
                   pl.BlockSpec((HG_SB, HG_HEADS, HG_DK, HG_DV), lambda j: (j, 0, 0, 0))],
        out_shape=[jax.ShapeDtypeStruct((N_SAMPLE, HG_WIDTH), _BF),
                   jax.ShapeDtypeStruct((DEC_BATCH, HG_HEADS, HG_DK, HG_DV), _F32)],
        compiler_params=_cparams(1),
        name="hgrn_sample",
    )(q, g, iv, sg, hg_norm_w, state)


def _route(h2, h2_bf, wr_hi_ref, wr_lo_ref, br_ref, route_ref, route_t_ref, cnt_ref):
    h2_lo = (h2 - h2_bf.astype(_F32)).astype(_BF)
    logits = (_dot(h2_bf, wr_hi_ref[...]) + (_dot(h2_lo, wr_hi_ref[...]) + _dot(h2_bf, wr_lo_ref[...]))
              + br_ref[...])
    lane = lax.broadcasted_iota(_I32, logits.shape, 1)
    neg = jnp.float32(-jnp.inf)
    logits = jnp.where(lane < N_EXPERTS, logits, neg)
    m1 = jnp.max(logits, axis=-1, keepdims=True)
    i1 = jnp.min(jnp.where(logits == m1, lane, LANES), axis=-1, keepdims=True)
    rest_l = jnp.where(lane == i1, neg, logits)
    m2 = jnp.max(rest_l, axis=-1, keepdims=True)
    i2 = jnp.min(jnp.where(rest_l == m2, lane, LANES), axis=-1, keepdims=True)
    e2 = jnp.exp(m2 - m1)
    w1 = 1.0 / (1.0 + e2)
    w2 = e2 / (1.0 + e2)
    hot1 = lane == i1
    hot2 = lane == i2
    tot = jnp.where(jnp.logical_or(hot1, hot2), 1.0, 0.0)
    trow = lax.broadcasted_iota(_I32, (TM, TM), 0)
    tcol = lax.broadcasted_iota(_I32, (TM, TM), 1)
    before = _dot((trow > tcol).astype(_BF), tot.astype(_BF))
    cnt = jnp.sum(tot, axis=0, keepdims=True)
    cnt_pad = jnp.ceil(cnt * (1.0 / RUN_ALIGN)) * RUN_ALIGN
    erow = lax.broadcasted_iota(_I32, (LANES, LANES), 0)
    ecol = lax.broadcasted_iota(_I32, (LANES, LANES), 1)
    start = _dot(jnp.broadcast_to(cnt_pad, (SUBLANES, LANES)).astype(_BF),
                 (erow < ecol).astype(_BF))[0:1, :]
    slot = before + start
    pos1 = jnp.sum(jnp.where(hot1, slot, 0.0), axis=-1, keepdims=True)
    pos2 = jnp.sum(jnp.where(hot2, slot, 0.0), axis=-1, keepdims=True)
    route = jnp.where(lane == 0, pos1, jnp.where(lane == 1, pos2,
                      jnp.where(lane == 2, w1, jnp.where(lane == 3, w2, 0.0))))
    route_ref[...] = route
    route_t_ref[...] = route.T[0:SUBLANES, :]
    cnt_ref[...] = jnp.broadcast_to(cnt, (SUBLANES, LANES)).astype(_I32)


def _mix_kernel(with_router, split_x, u_ref, v_ref, sga_ref, sgb_ref, op_ref, os_ref, *rest):
    if split_x:
        xp_ref, xs_ref, *rest = rest
        x = jnp.where(_is_sample_tile(), xs_ref[...], xp_ref[...])
    else:
        x_ref, *rest = rest
        x = x_ref[...]
    (mp_g1, ms_g1, mp_sh, ms_sh, mp_sc, ms_sc,
     wmix_ref, bmix_ref, wa_ref, wb_ref, wo_ref, npost_ref, npre_ref, *rest) = rest
    if with_router:
        wr_hi_ref, wr_lo_ref, br_ref, x1_ref, h2_ref, route_ref, route_t_ref, cnt_ref, a_scr = rest
    else:
        x1_ref, h2_ref, a_scr = rest
    for c in range(TM // GM_CHUNK):
        rows = slice(c * GM_CHUNK, (c + 1) * GM_CHUNK)
        for gi in range(GM_GROUPS):
            cols = slice(gi * GM_GROUP_DIM, (gi + 1) * GM_GROUP_DIM)
            mixed = _dot(wmix_ref[gi], v_ref[rows, cols])
            bias = bmix_ref[gi]
            mixed = mixed + jnp.concatenate([bias] * (GM_GROUP_DIM // LANES), axis=1)
            a_scr[rows, cols] = (u_ref[rows, cols].astype(_F32) * mixed).astype(_BF)
    br_a = _dot(a_scr[...], wa_ref[...])
    o = jnp.where(_is_sample_tile(), os_ref[...], op_ref[...])
    br_b = _dot(o, wb_ref[...])
    merged = sga_ref[...].astype(_F32) * br_a + sgb_ref[...].astype(_F32) * br_b
    mix = _dot(merged.astype(_BF), wo_ref[...])
    g1 = _mod_rows(mp_g1, ms_g1)
    nm = _rms(mix, npost_ref[...])
    x1 = x + _per_seq(nm, lambda t: t * g1[:, None, :])
    x1_ref[...] = x1
    h2 = _modulated_norm(x1, npre_ref, mp_sh, ms_sh, mp_sc, ms_sc)
    h2_bf = h2.astype(_BF)
    h2_ref[...] = h2_bf
    if with_router:
        _route(h2, h2_bf, wr_hi_ref, wr_lo_ref, br_ref, route_ref, route_t_ref, cnt_ref)


def _mix_call(layer, moe_idx, u, v, sga, sgb, o_p, o_s, x, modp, mods, wmix, bmix,
              wa, wb, wo, npost, npre, wr_hi=None, wr_lo=None, br=None):
    with_router = moe_idx is not None
    split_x = isinstance(x, tuple)
    ty = lambda i: (i >= NT_P).astype(_I32)
    wspec = pl.BlockSpec((None, D_MODEL, D_MODEL), lambda i: (layer, 0, 0))
    in_specs = [_tile_spec(), _tile_spec(), _tile_spec(), _tile_spec()] + _split_specs(HG_WIDTH)
    in_specs += _split_specs() if split_x else [_tile_spec()]
    in_specs += _mod_specs(layer, 2) + _mod_specs(layer, 3) + _mod_specs(layer, 4)
    in_specs += [pl.BlockSpec((None, None, GM_GROUPS, GM_CHUNK, GM_CHUNK), lambda i: (layer, ty(i), 0, 0, 0)),
                 pl.BlockSpec((None, None, GM_GROUPS, GM_CHUNK, LANES), lambda i: (layer, ty(i), 0, 0, 0)),
                 wspec, wspec, wspec, _row_spec(layer), _row_spec(layer)]
    args = [u, v, sga, sgb, o_p, o_s] + (list(x) if split_x else [x])
    args += [modp, mods, modp, mods, modp, mods, wmix, bmix, wa, wb, wo, npost, npre]
    out_specs = [_tile_spec(), _tile_spec()]
    out_shape = [jax.ShapeDtypeStruct((N_TOK, D_MODEL), _F32),
                 jax.ShapeDtypeStruct((N_TOK, D_MODEL), _BF)]
    if with_router:
        rspec = pl.BlockSpec((None, D_MODEL, LANES), lambda i: (moe_idx, 0, 0))
        in_specs += [rspec, rspec, pl.BlockSpec((None, 1, LANES), lambda i: (moe_idx, 0, 0))]
        args += [wr_hi, wr_lo, br]
        out_specs += [_tile_spec(LANES),
                      pl.BlockSpec((None, SUBLANES, TM), lambda i: (i, 0, 0)),
                      pl.BlockSpec((None, SUBLANES, LANES), lambda i: (i, 0, 0))]
        out_shape += [jax.ShapeDtypeStruct((N_TOK, LANES), _F32),
                      jax.ShapeDtypeStruct((NT, SUBLANES, TM), _F32),
                      jax.ShapeDtypeStruct((NT, SUBLANES, LANES), _I32)]
    return pl.pallas_call(
        functools.partial(_mix_kernel, with_router, split_x),
        grid=(NT,),
        in_specs=in_specs,
        out_specs=out_specs,
        out_shape=out_shape,
        scratch_shapes=[pltpu.VMEM((TM, GM_WIDTH), _BF)],
        compiler_params=_cparams(1),
        name="mix",
    )(*args)


def _ffn_partial(h, wg_ref, wu_ref, wd_ref):
    gate = _dot(h, wg_ref[...])
    up = _dot(h, wu_ref[...])
    act = (_silu(gate) * up).astype(_BF)
    return _dot(act, wd_ref[...])


def _layer_out(last, x1_ref, f, mp_g2, ms_g2, npost_ref, tail_refs):
    g2 = _mod_rows(mp_g2, ms_g2)
    nf = _rms(f, npost_ref[...])
    x2 = x1_ref[...] + _per_seq(nf, lambda t: t * g2[:, None, :])
    if last:
        yp_ref, ys_ref = tail_refs

        @pl.when(_is_sample_tile())
        def _():
            ys_ref[...] = x2

        @pl.when(jnp.logical_not(_is_sample_tile()))
        def _():
            yp_ref[...] = x2
    else:
        mp_sh, ms_sh, mp_sc, ms_sc, nw_ref, x_ref, h_ref = tail_refs
        x_ref[...] = x2
        h_ref[...] = _modulated_norm(x2, nw_ref, mp_sh, ms_sh, mp_sc, ms_sc).astype(_BF)


def _layer_out_specs(layer, last, modp, mods, npre_mix):
    if last:
        return [], [], _split_specs(), [jax.ShapeDtypeStruct((N_PROMPT, D_MODEL), _F32),
                                        jax.ShapeDtypeStruct((N_SAMPLE, D_MODEL), _F32)]
    in_specs = _mod_specs(layer + 1, 0) + _mod_specs(layer + 1, 1) + [_row_spec(layer + 1)]
    return (in_specs, [modp, mods, modp, mods, npre_mix], [_tile_spec(), _tile_spec()],
            [jax.ShapeDtypeStruct((N_TOK, D_MODEL), _F32), jax.ShapeDtypeStruct((N_TOK, D_MODEL), _BF)])


def _ffn_kernel(last, h_ref, x1_ref, mp_g2, ms_g2, wg_ref, wu_ref, wd_ref, npost_ref, *rest):
    *tail_refs, acc_ref = rest
    k = pl.program_id(1)
    part = _ffn_partial(h_ref[...], wg_ref, wu_ref, wd_ref)

    @pl.when(k == 0)
    def _():
        acc_ref[...] = part

    @pl.when(k > 0)
    def _():
        acc_ref[...] += part

    @pl.when(k == pl.num_programs(1) - 1)
    def _():
        _layer_out(last, x1_ref, acc_ref[...], mp_g2, ms_g2, npost_ref, tail_refs)


def _ffn_call(layer, j, h2, x1, modp, mods, wg, wu, wd, npost, npre_mix):
    nk = D_FF // FF_CHUNK
    last = layer == DEPTH - 1
    tile = pl.BlockSpec((TM, D_MODEL), lambda i, k: (i, 0))
    extra_specs, extra_args, out_specs, out_shape = _layer_out_specs(layer, last, modp, mods, npre_mix)
    return pl.pallas_call(
        functools.partial(_ffn_kernel, last),
        grid=(NT, nk),
        in_specs=[tile, tile] + _mod_specs(layer, 5) + [
            pl.BlockSpec((None, D_MODEL, FF_CHUNK), lambda i, k: (j, 0, k)),
            pl.BlockSpec((None, D_MODEL, FF_CHUNK), lambda i, k: (j, 0, k)),
            pl.BlockSpec((None, FF_CHUNK, D_MODEL), lambda i, k: (j, k, 0)),
            pl.BlockSpec((None, 1, D_MODEL), lambda i, k: (layer, 0, 0))] + extra_specs,
        out_specs=out_specs,
        out_shape=out_shape,
        scratch_shapes=[pltpu.VMEM((TM, D_MODEL), _F32)],
        compiler_params=_cparams(2),
        name="ffn",
    )(h2, x1, modp, mods, wg, wu, wd, npost, *extra_args)


def _route_tables(cnt):
    cnt = cnt[:, 0, :N_EXPERTS]
    n = (cnt + (RUN_ALIGN - 1)) // RUN_ALIGN * RUN_ALIGN
    s = jnp.cumsum(n, axis=1) - n
    rows_e = jnp.sum(n, axis=0)
    region = (rows_e + (TG - 1)) // TG * TG
    region_end = jnp.cumsum(region)
    off = region_end - region
    p = off[None, :] + jnp.cumsum(n, axis=0) - n
    n_tiles = region_end[-1] // TG
    tile_row0 = jnp.minimum(jnp.arange(R_TILES, dtype=_I32), n_tiles - 1) * TG
    tile_expert = jnp.sum((tile_row0[:, None] >= region_end[None, :]).astype(_I32), axis=1)
    flat = lambda a: a.reshape(-1).astype(_I32)
    return dict(p=flat(p), s=flat(s), n=flat(n), tail_p=flat(off + rows_e), tail_n=flat(region - rows_e),
                tile_expert=flat(tile_expert), n_tiles=flat(n_tiles))


def _for_each_run_piece(n, src0, dst0, fn):
    for size in RUN_SIZES:
        done = n & (-2 * size)

        @pl.when((n & size) != 0)
        def _():
            fn(pl.multiple_of(src0 + done, RUN_ALIGN), pl.multiple_of(dst0 + done, RUN_ALIGN), size)


def _dispatch_kernel(p_tab, s_tab, n_tab, tail_p, tail_n, nt_ref, h_ref, rt_ref, xs_ref,
                     comp_scr, zero_scr, sem):
    t = pl.program_id(0)
    pos1 = rt_ref[0:1, :].astype(_I32)
    pos2 = rt_ref[1:2, :].astype(_I32)
    r = lax.broadcasted_iota(_I32, (CB, TM), 0)
    perm = jnp.where(jnp.logical_or(r == pos1, r == pos2), 1.0, 0.0).astype(_BF)
    comp_scr[...] = _dot(perm, h_ref[...]).astype(_BF)

    def runs(go):
        for e in range(N_EXPERTS):
            idx = t * N_EXPERTS + e

            def piece(src, dst, size):
                go(pltpu.make_async_copy(comp_scr.at[pl.ds(src, size)], xs_ref.at[pl.ds(dst, size)], sem))

            _for_each_run_piece(n_tab[idx], s_tab[idx], p_tab[idx], piece)

    runs(lambda cp: cp.start())
    runs(lambda cp: cp.wait())

    @pl.when(t == NT - 1)
    def _():
        zero_scr[...] = jnp.zeros_like(zero_scr)

        def tails(go):
            for e in range(N_EXPERTS):
                def piece(src, dst, size):
                    go(pltpu.make_async_copy(zero_scr.at[pl.ds(0, size)], xs_ref.at[pl.ds(dst, size)], sem))

                _for_each_run_piece(tail_n[e], 0, tail_p[e], piece)

        tails(lambda cp: cp.start())
        tails(lambda cp: cp.wait())

        def spare_tile(go):
            def body(r, carry):
                dst = pl.multiple_of(r * TG, TG)
                go(pltpu.make_async_copy(zero_scr, xs_ref.at[pl.ds(dst, TG)], sem))
                return carry
            lax.fori_loop(nt_ref[0], R_TILES, body, 0)

        spare_tile(lambda cp: cp.start())
        spare_tile(lambda cp: cp.wait())


def _dispatch_call(tabs, h2, route_t):
    grid_spec = pltpu.PrefetchScalarGridSpec(
        num_scalar_prefetch=6,
        grid=(NT,),
        in_specs=[_tile_spec(), pl.BlockSpec((None, SUBLANES, TM), lambda i, *_: (i, 0, 0))],
        out_specs=pl.BlockSpec(memory_space=pl.ANY),
        scratch_shapes=[pltpu.VMEM((CB, D_MODEL), _BF), pltpu.VMEM((TG, D_MODEL), _BF),
                        pltpu.SemaphoreType.DMA(())])
    return pl.pallas_call(
        _dispatch_kernel,
        grid_spec=grid_spec,
        out_shape=jax.ShapeDtypeStruct((N_ROWS, D_MODEL), _BF),
        compiler_params=_cparams(1),
        name="dispatch",
    )(tabs["p"], tabs["s"], tabs["n"], tabs["tail_p"], tabs["tail_n"], tabs["n_tiles"], h2, route_t)


def _expert_kernel(te_ref, nt_ref, x_ref, wg_ref, wu_ref, wd_ref, y_ref, acc_ref):
    r = pl.program_id(0)
    k = pl.program_id(1)

    @pl.when(r < nt_ref[0])
    def _():
        part = _ffn_partial(x_ref[...], wg_ref, wu_ref, wd_ref)

        @pl.when(k == 0)
        def _():
            acc_ref[...] = part

        @pl.when(k > 0)
        def _():
            acc_ref[...] += part

        @pl.when(k == pl.num_programs(1) - 1)
        def _():
            y_ref[...] = acc_ref[...].astype(_BF)

    @pl.when(r >= nt_ref[0])
    def _():
        y_ref[...] = jnp.zeros_like(y_ref)


def _expert_call(j, tabs, xs, wg, wu, wd):
    nk = D_FF // FF_CHUNK
    row_blk = lambda r, k, te, nt: (jnp.minimum(r, nt[0] - 1), 0)
    kk = lambda r, k, nt: jnp.where(r < nt[0], k, nk - 1)
    grid_spec = pltpu.PrefetchScalarGridSpec(
        num_scalar_prefetch=2,
        grid=(R_TILES, nk),
        in_specs=[pl.BlockSpec((TG, D_MODEL), row_blk),
                  pl.BlockSpec((None, None, D_MODEL, FF_CHUNK), lambda r, k, te, nt: (j, te[r], 0, kk(r, k, nt))),
                  pl.BlockSpec((None, None, D_MODEL, FF_CHUNK), lambda r, k, te, nt: (j, te[r], 0, kk(r, k, nt))),
                  pl.BlockSpec((None, None, FF_CHUNK, D_MODEL), lambda r, k, te, nt: (j, te[r], kk(r, k, nt), 0))],
        out_specs=pl.BlockSpec((TG, D_MODEL), lambda r, k, te, nt: (r, 0)),
        scratch_shapes=[pltpu.VMEM((TG, D_MODEL), _F32)])
    return pl.pallas_call(
        _expert_kernel,
        grid_spec=grid_spec,
        out_shape=jax.ShapeDtypeStruct((N_ROWS, D_MODEL), _BF),
        compiler_params=_cparams(2),
        name="experts",
    )(tabs["tile_expert"], tabs["n_tiles"], xs, wg, wu, wd)


def _combine_kernel(last, p_tab, s_tab, n_tab, route_ref, x1_ref, mp_g2, ms_g2, npost_ref, ys_ref, *rest):
    *tail_refs, yc_scr, sem = rest
    t = pl.program_id(0)
    yc_scr[...] = jnp.zeros_like(yc_scr)

    def runs(go):
        for e in range(N_EXPERTS):
            idx = t * N_EXPERTS + e

            def piece(src, dst, size):
                go(pltpu.make_async_copy(ys_ref.at[pl.ds(dst, size)], yc_scr.at[pl.ds(src, size)], sem))

            _for_each_run_piece(n_tab[idx], s_tab[idx], p_tab[idx], piece)

    runs(lambda cp: cp.start())
    runs(lambda cp: cp.wait())
    route = route_ref[...]
    pos1 = route[:, 0:1].astype(_I32)
    pos2 = route[:, 1:2].astype(_I32)
    w1 = route[:, 2:3]
    w2 = route[:, 3:4]
    slot = lax.broadcasted_iota(_I32, (TM, CB), 1)
    yc = yc_scr[...]
    pick1 = jnp.where(slot == pos1, 1.0, 0.0).astype(_BF)
    pick2 = jnp.where(slot == pos2, 1.0, 0.0).astype(_BF)
    f = w1 * _dot(pick1, yc) + w2 * _dot(pick2, yc)
    _layer_out(last, x1_ref, f, mp_g2, ms_g2, npost_ref, tail_refs)


def _combine_call(layer, tabs, route, x1, modp, mods, npost, ys, npre_mix):
    last = layer == DEPTH - 1
    extra_specs, extra_args, out_specs, out_shape = _layer_out_specs(layer, last, modp, mods, npre_mix)
    grid_spec = pltpu.PrefetchScalarGridSpec(
        num_scalar_prefetch=3,
        grid=(NT,),
        in_specs=[_tile_spec(LANES), _tile_spec()] + _mod_specs(layer, 5) + [
            _row_spec(layer), pl.BlockSpec(memory_space=pl.ANY)] + extra_specs,
        out_specs=out_specs,
        scratch_shapes=[pltpu.VMEM((CB, D_MODEL), _BF), pltpu.SemaphoreType.DMA(())])
    return pl.pallas_call(
        functools.partial(_combine_kernel, last),
        grid_spec=grid_spec,
        out_shape=out_shape,
        compiler_params=_cparams(1),
        name="combine",
    )(tabs["p"], tabs["s"], tabs["n"], route, x1, modp, mods, npost, ys, *extra_args)


def _spatial_tables(gm_ws, gm_bs):
    mask_p = jnp.tril(jnp.ones((GM_CHUNK, GM_CHUNK), bool))
    w_p = jnp.where(mask_p, gm_ws, 0.0)
    mask_s = jnp.tril(jnp.ones((DEC_SEQ, DEC_SEQ), bool))
    w_small = jnp.where(mask_s, gm_ws[:, :, :DEC_SEQ, :DEC_SEQ], 0.0)
    eye = jnp.eye(GM_CHUNK // DEC_SEQ, dtype=gm_ws.dtype)
    w_s = jnp.einsum("ab,lgts->lgatbs", eye, w_small).reshape(gm_ws.shape)
    b_p = gm_bs
    b_s = jnp.tile(gm_bs[:, :, :DEC_SEQ], (1, 1, GM_CHUNK // DEC_SEQ))
    wmix = jnp.stack([w_p, w_s], axis=1).astype(_BF)
    bmix = jnp.stack([b_p, b_s], axis=1)[..., None]
    bmix = jnp.broadcast_to(bmix, bmix.shape[:-1] + (LANES,)).astype(_F32)
    return wmix, bmix


def kernel(x_prompt, x_sample, c_prompt, c_sample, state_hgrn, w_in, gm_ln_w, gm_ln_b, gm_ws, gm_bs,
           hg_lb_raw, hg_norm_w, w_branch_a, w_branch_b, w_out, w_ada, b_ada,
           norm_pre_mix, norm_post_mix, norm_pre_ffn, norm_post_ffn,
           w_ffn_gate, w_ffn_up, w_ffn_down, w_router, b_router, w_exp_gate, w_exp_up, w_exp_down):
    x = (x_prompt.reshape(N_PROMPT, D_MODEL), x_sample.reshape(N_SAMPLE, D_MODEL))
    c_all = jnp.concatenate([c_prompt, c_sample], axis=0)
    mod = _ada_call(c_all, w_ada, b_ada)
    modp = mod[:, :BATCH].reshape(DEPTH, BATCH, 1, 6 * D_MODEL)
    mods = mod[:, BATCH:]
    loglb = _lb_call(hg_lb_raw)
    aux = jnp.zeros((DEPTH, SUBLANES, D_MODEL), _F32)
    aux = aux.at[:, 0].set(gm_ln_w).at[:, 1].set(gm_ln_b).at[:, 2].set(loglb)
    wmix, bmix = _spatial_tables(gm_ws, gm_bs)
    row3 = lambda a: a.reshape(a.shape[0], 1, a.shape[1])
    w_in_bf = w_in.astype(_BF)
    wa_bf, wb_bf, wo_bf = w_branch_a.astype(_BF), w_branch_b.astype(_BF), w_out.astype(_BF)
    wfg, wfu, wfd = w_ffn_gate.astype(_BF), w_ffn_up.astype(_BF), w_ffn_down.astype(_BF)
    weg, weu, wed = w_exp_gate.astype(_BF), w_exp_up.astype(_BF), w_exp_down.astype(_BF)
    wr_pad = jnp.pad(w_router, ((0, 0), (0, 0), (0, LANES - N_EXPERTS)))
    wr_hi = wr_pad.astype(_BF)
    wr_lo = (wr_pad - wr_hi.astype(_F32)).astype(_BF)
    br_pad = jnp.pad(b_router, ((0, 0), (0, LANES - N_EXPERTS))).reshape(-1, 1, LANES)
    npre_mix, npost_mix = row3(norm_pre_mix), row3(norm_post_mix)
    npre_ffn, npost_ffn = row3(norm_pre_ffn), row3(norm_post_ffn)
    hg_nw = row3(hg_norm_w)

    hp, hs, vs = [], [], []
    h = _prenorm_call(0, x[0], x[1], modp, mods, npre_mix)
    for l in range(DEPTH):
        u, v, v32, q, g = _inproj_call(l, 0, ("gelu", "gelu_ln", "silu", "logf"), h, w_in_bf, aux)
        iv, sg, sga, sgb = _inproj_call(l, 4, ("id", "silu", "sigmoid", "sigmoid"), h, w_in_bf, aux)
        o_p, s_p = _hgrn_prompt_call(l, q, g, iv, sg, hg_nw)
        o_s, s_s = _hgrn_sample_call(l, q, g, iv, sg, hg_nw, state_hgrn)
        j = l // 2
        if l % 2 == 0:
            x1, h2 = _mix_call(l, None, u, v, sga, sgb, o_p, o_s, x, modp, mods, wmix, bmix,
                               wa_bf, wb_bf, wo_bf, npost_mix, npre_ffn)
            out = _ffn_call(l, j, h2, x1, modp, mods, wfg, wfu, wfd, npost_ffn, npre_mix)
        else:
            x1, h2, route, route_t, cnt = _mix_call(l, j, u, v, sga, sgb, o_p, o_s, x, modp, mods, wmix, bmix,
                                                    wa_bf, wb_bf, wo_bf, npost_mix, npre_ffn,
                                                    wr_hi, wr_lo, br_pad)
            tabs = _route_tables(cnt)
            xs = _dispatch_call(tabs, h2, route_t)
            ys = _expert_call(j, tabs, xs, weg, weu, wed)
            out = _combine_call(l, tabs, route, x1, modp, mods, npost_ffn, ys, npre_mix)
        if l < DEPTH - 1:
            x, h = out
        hp.append(s_p)
        hs.append(s_s)
        vs.append(v32.reshape(DEC_BATCH, DEC_SEQ, GM_WIDTH))
    y_prompt = out[0].reshape(BATCH, SEQ, D_MODEL)
    y_sample = out[1].reshape(DEC_BATCH, DEC_SEQ, D_MODEL)
    return (y_prompt, y_sample, jnp.stack(hp, axis=0), jnp.stack(hs, axis=0), jnp.stack(vs, axis=0))
```

```python
import functools
import math

import jax
import jax.numpy as jnp
from jax import lax
from jax.experimental import pallas as pl
from jax.experimental.pallas import tpu as pltpu

D_MODEL = 1024
BATCH = 8
SEQ = 2048
DEPTH = 4
DEC_BATCH = 128
DEC_SEQ = 8
GM_WIDTH = 1024
GM_GROUPS = 4
GM_GROUP_DIM = GM_WIDTH // GM_GROUPS
GM_CHUNK = 128
HG_HEADS = 8
HG_DK = 128
HG_DV = 128
HG_WIDTH = HG_HEADS * HG_DK
D_FF = 2816
N_EXPERTS = 8
TOP_K = 2
IN_COLS = 2 * GM_WIDTH + 4 * HG_WIDTH + 2 * D_MODEL
EPS = 1e-6

LANES = 128
SUBLANES = 8
BF16_ROWS = 16
TM = 512
N_PROMPT = BATCH * SEQ
N_SAMPLE = DEC_BATCH * DEC_SEQ
N_TOK = N_PROMPT + N_SAMPLE
NT_P = N_PROMPT // TM
NT_S = N_SAMPLE // TM
NT = NT_P + NT_S
TILES_PER_SEQ = SEQ // TM
SEQ_PER_TILE = TM // DEC_SEQ
HG_C = 64
HG_BLK = 256
HG_SB = 8
FF_CHUNK = D_FF // 2
INPROJ_ROWS = 256
EXP_RANGE = 70.0
VMEM_LIMIT = 56 * 1024 * 1024

TG = 512
RUN_ALIGN = BF16_ROWS
RUN_SIZES = tuple(TM >> k for k in range((TM // RUN_ALIGN).bit_length()))
CB = TOP_K * TM + N_EXPERTS * RUN_ALIGN
_ROWS_BOUND = TOP_K * N_TOK + NT * N_EXPERTS * (RUN_ALIGN - 1) + N_EXPERTS * (TG - RUN_ALIGN)
R_TILES = -(-_ROWS_BOUND // TG)
N_ROWS = R_TILES * TG

_BF = jnp.bfloat16
_F32 = jnp.float32
_I32 = jnp.int32


def _cparams(n_axes):
    return pltpu.CompilerParams(dimension_semantics=("arbitrary",) * n_axes,
                                vmem_limit_bytes=VMEM_LIMIT)


def _dot(a, b):
    return jnp.dot(a, b, preferred_element_type=_F32)


def _sigmoid(x):
    return 0.5 + 0.5 * jnp.tanh(0.5 * x)


def _silu(x):
    hx = 0.5 * x
    return hx + hx * jnp.tanh(hx)


def _gelu(x):
    return 0.5 * x * (1.0 + lax.erf(x * (1.0 / math.sqrt(2.0))))


def _rms(x, w):
    return x * lax.rsqrt(jnp.mean(x * x, axis=-1, keepdims=True) + EPS) * w


def _is_sample_tile():
    return pl.program_id(0) >= NT_P


def _mod_rows(mp_ref, ms_ref):
    mp = jnp.broadcast_to(mp_ref[...], ms_ref.shape)
    return jnp.where(_is_sample_tile(), ms_ref[...], mp)


def _per_seq(x, fn_rows):
    x3 = x.reshape(SEQ_PER_TILE, DEC_SEQ, x.shape[-1])
    return fn_rows(x3).reshape(TM, x.shape[-1])


def _mod_specs(layer, col_block):
    mp = pl.BlockSpec((None, None, 1, D_MODEL),
                      lambda i, *_: (layer, jnp.minimum(i // TILES_PER_SEQ, BATCH - 1), 0, col_block))
    ms = pl.BlockSpec((None, SEQ_PER_TILE, D_MODEL),
                      lambda i, *_: (layer, jnp.maximum(i - NT_P, 0), col_block))
    return [mp, ms]


def _tile_spec(width=D_MODEL):
    return pl.BlockSpec((TM, width), lambda i, *_: (i, 0))


def _row_spec(layer, width=D_MODEL):
    return pl.BlockSpec((None, 1, width), lambda i, *_: (layer, 0, 0))


def _ada_kernel(c_ref, w_ref, b_ref, o_ref):
    s = _silu(c_ref[...]).astype(_BF)
    o_ref[...] = _dot(s, w_ref[...].astype(_BF)) + b_ref[...]


def _ada_call(c_all, w_ada, b_ada):
    nb = 4
    wcol = 6 * D_MODEL // nb
    rows = c_all.shape[0]
    return pl.pallas_call(
        _ada_kernel,
        grid=(DEPTH, nb),
        in_specs=[pl.BlockSpec((rows, D_MODEL), lambda l, j: (0, 0)),
                  pl.BlockSpec((None, D_MODEL, wcol), lambda l, j: (l, 0, j)),
                  pl.BlockSpec((None, 1, wcol), lambda l, j: (l, 0, j))],
        out_specs=pl.BlockSpec((None, rows, wcol), lambda l, j: (l, 0, j)),
        out_shape=jax.ShapeDtypeStruct((DEPTH, rows, 6 * D_MODEL), _F32),
        compiler_params=_cparams(2),
        name="ada",
    )(c_all, w_ada, b_ada.reshape(DEPTH, 1, 6 * D_MODEL))


def _lb_kernel(raw_ref, loglb_ref):
    raw = raw_ref[...]
    m = jnp.max(raw, axis=0, keepdims=True)
    e = jnp.exp(raw - m)
    p = e / jnp.sum(e, axis=0, keepdims=True)
    acc = jnp.zeros((1, HG_WIDTH), _F32)
    for l in range(DEPTH):
        if l > 0:
            acc = acc + p[l:l + 1, :]
        loglb_ref[l:l + 1, :] = jnp.log(acc)


def _lb_call(raw):
    return pl.pallas_call(
        _lb_kernel,
        out_shape=jax.ShapeDtypeStruct((DEPTH, HG_WIDTH), _F32),
        name="lower_bounds",
    )(raw)


def _modulated_norm(x, nw_ref, mp_sh, ms_sh, mp_sc, ms_sc):
    xn = _rms(x, nw_ref[...])
    sh = _mod_rows(mp_sh, ms_sh)
    sc = _mod_rows(mp_sc, ms_sc)
    return _per_seq(xn, lambda x3: x3 * (1.0 + sc)[:, None, :] + sh[:, None, :])


def _prenorm_kernel(xp_ref, xs_ref, mp_sh, ms_sh, mp_sc, ms_sc, nw_ref, h_ref):
    x = jnp.where(_is_sample_tile(), xs_ref[...], xp_ref[...])
    h_ref[...] = _modulated_norm(x, nw_ref, mp_sh, ms_sh, mp_sc, ms_sc).astype(_BF)


def _split_specs(width=D_MODEL):
    return [pl.BlockSpec((TM, width), lambda i, *_: (jnp.minimum(i, NT_P - 1), 0)),
            pl.BlockSpec((TM, width), lambda i, *_: (jnp.maximum(i - NT_P, 0), 0))]


def _prenorm_call(layer, x_p, x_s, modp, mods, norm_w):
    return pl.pallas_call(
        _prenorm_kernel,
        grid=(NT,),
        in_specs=_split_specs() + _mod_specs(layer, 0) + _mod_specs(layer, 1) + [_row_spec(layer)],
        out_specs=_tile_spec(),
        out_shape=jax.ShapeDtypeStruct((N_TOK, D_MODEL), _BF),
        compiler_params=_cparams(1),
        name="prenorm",
    )(x_p, x_s, modp, mods, modp, mods, norm_w)


def _inproj_kernel(kinds, h_ref, w_ref, aux_ref, *out_refs):
    for rb in range(TM // INPROJ_ROWS):
        rs = slice(rb * INPROJ_ROWS, (rb + 1) * INPROJ_ROWS)
        h = h_ref[rs, :]
        outs = list(out_refs)
        for j, kind in enumerate(kinds):
            z = _dot(h, w_ref[:, j * D_MODEL:(j + 1) * D_MODEL])
            if kind == "gelu":
                outs.pop(0)[rs, :] = _gelu(z).astype(_BF)
            elif kind == "gelu_ln":
                a = _gelu(z)
                mu = jnp.mean(a, axis=-1, keepdims=True)
                ac = a - mu
                vn = ac * lax.rsqrt(jnp.mean(ac * ac, axis=-1, keepdims=True) + EPS)
                vn = vn * aux_ref[0:1, :] + aux_ref[1:2, :]
                outs.pop(0)[rs, :] = vn.astype(_BF)
                v32_ref = outs.pop(0)

                @pl.when(_is_sample_tile())
                def _():
                    v32_ref[rs, :] = vn
            elif kind == "silu":
                outs.pop(0)[rs, :] = _silu(z).astype(_BF)
            elif kind == "sigmoid":
                outs.pop(0)[rs, :] = _sigmoid(z).astype(_BF)
            elif kind == "id":
                outs.pop(0)[rs, :] = z.astype(_BF)
            elif kind == "logf":
                loglb = aux_ref[2:3, :]
                tail = jnp.log(1.0 + jnp.exp(-jnp.abs(z)))
                a = jnp.minimum(z, 0.0) - tail
                c = loglb - jnp.maximum(z, 0.0) - tail
                outs.pop(0)[rs, :] = jnp.maximum(a, c) + jnp.log(1.0 + jnp.exp(-jnp.abs(z - loglb)))
            else:
                raise ValueError(kind)


def _inproj_call(layer, sec0, kinds, h, w_in_bf, aux):
    nsec = len(kinds)
    out_specs, out_shape = [], []
    for kind in kinds:
        dt = _F32 if kind == "logf" else _BF
        out_specs.append(_tile_spec())
        out_shape.append(jax.ShapeDtypeStruct((N_TOK, D_MODEL), dt))
        if kind == "gelu_ln":
            out_specs.append(pl.BlockSpec((TM, D_MODEL), lambda i: (jnp.maximum(i - NT_P, 0), 0)))
            out_shape.append(jax.ShapeDtypeStruct((N_SAMPLE, D_MODEL), _F32))
    return pl.pallas_call(
        functools.partial(_inproj_kernel, kinds),
        grid=(NT,),
        in_specs=[_tile_spec(),
                  pl.BlockSpec((None, D_MODEL, nsec * D_MODEL), lambda i: (layer, 0, sec0 // nsec)),
                  pl.BlockSpec((None, SUBLANES, D_MODEL), lambda i: (layer, 0, 0))],
        out_specs=out_specs,
        out_shape=out_shape,
        compiler_params=_cparams(1),
        name=f"inproj_{sec0}",
    )(h, w_in_bf, aux)


def _col_bcast(row):
    n = row.shape[-1]
    return jnp.broadcast_to(row, (n, n)).T


def _head_out(o, nw, sg):
    return (_rms(o, nw) * sg.astype(_F32)).astype(_BF)


def _hgrn_rows8(rows, q_ref, g_ref, i_ref, sg_ref, nw_ref, o_ref, s_get, s_put):
    t_row = lax.broadcasted_iota(_I32, (DEC_SEQ, HG_WIDTH), 0)
    t_col = lax.broadcasted_iota(_I32, (DEC_SEQ, 1), 0)
    g = g_ref[rows, :]
    b = g
    shift = 1
    while shift < DEC_SEQ:
        b = b + jnp.where(t_row >= shift, pltpu.roll(b, shift, axis=0), 0.0)
        shift *= 2
    kk = 1.0 - jnp.exp(g)
    q = q_ref[rows, :].astype(_F32)
    bend = b[DEC_SEQ - 1:DEC_SEQ, :]
    qe = (q * jnp.exp(b)).astype(_BF)
    kdec = (kk * jnp.exp(bend - b)).astype(_BF)
    e_end = jnp.exp(bend)
    iv_all = i_ref[rows, :]
    sg_all = sg_ref[rows, :]
    for h in range(HG_HEADS):
        hs = slice(h * HG_DK, (h + 1) * HG_DK)
        qh, bh, kh = q[:, hs], b[:, hs], kk[:, hs]
        iv = iv_all[:, hs]
        ivf = iv.astype(_F32)
        s_old = s_get(h)
        o = _dot(qe[:, hs], s_old.astype(_BF))
        for j in range(DEC_SEQ):
            decay = jnp.exp(jnp.minimum(bh - bh[j:j + 1, :], 0.0))
            w = jnp.sum(qh * decay * kh[j:j + 1, :], axis=-1, keepdims=True)
            o = o + jnp.where(t_col >= j, w, 0.0) * ivf[j:j + 1, :]
        upd = lax.dot_general(kdec[:, hs], iv, (((0,), (0,)), ((), ())),
                              preferred_element_type=_F32)
        s_put(h, _col_bcast(e_end[:, hs]) * s_old + upd)
        o_ref[rows, hs] = _head_out(o, nw_ref[...], sg_all[:, hs])


def _rows_of_chunks(x, row_in_chunk):
    parts = []
    for c in range(HG_BLK // HG_C):
        r = c * HG_C + row_in_chunk
        parts.append(jnp.broadcast_to(x[r:r + 1, :], (HG_C, x.shape[-1])))
    return jnp.concatenate(parts, axis=0)


def _hgrn_prompt_kernel(q_ref, g_ref, i_ref, sg_ref, nw_ref, o_ref, s_out_ref, s_scr):
    c = pl.program_id(1)

    @pl.when(c == 0)
    def _():
        s_scr[...] = jnp.zeros_like(s_scr)

    g = g_ref[...]
    row = lax.broadcasted_iota(_I32, (HG_BLK, HG_BLK), 0)
    col = lax.broadcasted_iota(_I32, (HG_BLK, HG_BLK), 1)
    tril = jnp.logical_and(row >= col, row // HG_C == col // HG_C).astype(_BF)
    g_hi = g.astype(_BF)
    r1 = g - g_hi.astype(_F32)
    g_mid = r1.astype(_BF)
    g_lo = (r1 - g_mid.astype(_F32)).astype(_BF)
    b = _dot(tril, g_hi) + _dot(tril, g_mid) + _dot(tril, g_lo)
    ref = _rows_of_chunks(b, HG_C // 2 - 1)
    in_range = jnp.max(jnp.abs(b - ref)) <= EXP_RANGE

    @pl.when(in_range)
    def _():
        kk = 1.0 - jnp.exp(g)
        q = q_ref[...].astype(_F32)
        bend = _rows_of_chunks(b, HG_C - 1)
        qt = (q * jnp.exp(b - ref)).astype(_BF)
        kt = (kk * jnp.exp(ref - b)).astype(_BF)
        qe = (q * jnp.exp(b)).astype(_BF)
        kdec = (kk * jnp.exp(bend - b)).astype(_BF)
        e_end = jnp.exp(bend)
        crow = lax.broadcasted_iota(_I32, (HG_C, HG_C), 0)
        ccol = lax.broadcasted_iota(_I32, (HG_C, HG_C), 1)
        causal = crow >= ccol
        for h in range(HG_HEADS):
            hs = slice(h * HG_DK, (h + 1) * HG_DK)
            s_cur = s_scr[h]
            for ci in range(HG_BLK // HG_C):
                rs = slice(ci * HG_C, (ci + 1) * HG_C)
                iv = i_ref[rs, hs]
                scores = lax.dot_general(qt[rs, hs], kt[rs, hs], (((1,), (1,)), ((), ())),
                                         preferred_element_type=_F32)
                scores = jnp.where(causal, scores, 0.0).astype(_BF)
                o = _dot(scores, iv) + _dot(qe[rs, hs], s_cur.astype(_BF))
                upd = lax.dot_general(kdec[rs, hs], iv, (((0,), (0,)), ((), ())),
                                      preferred_element_type=_F32)
                s_cur = _col_bcast(e_end[ci * HG_C:ci * HG_C + 1, hs]) * s_cur + upd
                o_ref[rs, hs] = _head_out(o, nw_ref[...], sg_ref[rs, hs])
            s_scr[h] = s_cur

    @pl.when(jnp.logical_not(in_range))
    def _():
        def s_put(h, val):
            s_scr[h] = val

        def sub_chunk(k, carry):
            rows = pl.ds(pl.multiple_of(k * DEC_SEQ, DEC_SEQ), DEC_SEQ)
            _hgrn_rows8(rows, q_ref, g_ref, i_ref, sg_ref, nw_ref, o_ref, lambda h: s_scr[h], s_put)
            return carry

        lax.fori_loop(0, HG_BLK // DEC_SEQ, sub_chunk, 0)

    @pl.when(c == pl.num_programs(1) - 1)
    def _():
        s_out_ref[...] = s_scr[...]


def _hgrn_prompt_call(layer, q, g, iv, sg, hg_norm_w):
    nc = SEQ // HG_BLK
    blk = pl.BlockSpec((HG_BLK, HG_WIDTH), lambda b, c: (b * nc + c, 0))
    return pl.pallas_call(
        _hgrn_prompt_kernel,
        grid=(BATCH, nc),
        in_specs=[blk, blk, blk, blk,
                  pl.BlockSpec((None, 1, HG_DV), lambda b, c: (layer, 0, 0))],
        out_specs=[blk,
                   pl.BlockSpec((None, HG_HEADS, HG_DK, HG_DV), lambda b, c: (b, 0, 0, 0))],
        out_shape=[jax.ShapeDtypeStruct((N_PROMPT, HG_WIDTH), _BF),
                   jax.ShapeDtypeStruct((BATCH, HG_HEADS, HG_DK, HG_DV), _F32)],
        scratch_shapes=[pltpu.VMEM((HG_HEADS, HG_DK, HG_DV), _F32)],
        compiler_params=_cparams(2),
        name="hgrn_prompt",
    )(q, g, iv, sg, hg_norm_w)


def _hgrn_sample_kernel(q_ref, g_ref, i_ref, sg_ref, nw_ref, s_in_ref, all_states_ref, o_ref, s_out_ref):
    del all_states_ref
    def seq_body(s, carry):
        rows = pl.ds(pl.multiple_of(s * DEC_SEQ, DEC_SEQ), DEC_SEQ)

        def s_put(h, val):
            s_out_ref[s, h] = val

        _hgrn_rows8(rows, q_ref, g_ref, i_ref, sg_ref, nw_ref, o_ref, lambda h: s_in_ref[s, h], s_put)
        return carry

    lax.fori_loop(0, HG_SB, seq_body, 0, unroll=2)


def _hgrn_sample_call(layer, q, g, iv, sg, hg_norm_w, state, all_states):
    rows = HG_SB * DEC_SEQ
    off = N_PROMPT // rows
    blk = pl.BlockSpec((rows, HG_WIDTH), lambda j: (off + j, 0))
    s_blk = pl.BlockSpec((None, HG_SB, HG_HEADS, HG_DK, HG_DV), lambda j: (layer, j, 0, 0, 0))
    return pl.pallas_call(
        _hgrn_sample_kernel,
        grid=(DEC_BATCH // HG_SB,),
        in_specs=[blk, blk, blk, blk,
                  pl.BlockSpec((None, 1, HG_DV), lambda j: (layer, 0, 0)),
                  s_blk, pl.BlockSpec(memory_space=pl.ANY)],
        out_specs=[pl.BlockSpec((rows, HG_WIDTH), lambda j: (j, 0)), s_blk],
        out_shape=[jax.ShapeDtypeStruct((N_SAMPLE, HG_WIDTH), _BF),
                   jax.ShapeDtypeStruct(all_states.shape, _F32)],
        input_output_aliases={6: 1},
        compiler_params=_cparams(1),
        name="hgrn_sample",
    )(q, g, iv, sg, hg_norm_w, state, all_states)


def _route(h2, h2_bf, wr_hi_ref, wr_lo_ref, br_ref, route_ref, route_t_ref, cnt_ref):
    h2_lo = (h2 - h2_bf.astype(_F32)).astype(_BF)
    logits = (_dot(h2_bf, wr_hi_ref[...]) + (_dot(h2_lo, wr_hi_ref[...]) + _dot(h2_bf, wr_lo_ref[...]))
              + br_ref[...])
    lane = lax.broadcasted_iota(_I32, logits.shape, 1)
    neg = jnp.float32(-jnp.inf)
    logits = jnp.where(lane < N_EXPERTS, logits, neg)
    m1 = jnp.max(logits, axis=-1, keepdims=True)
    i1 = jnp.min(jnp.where(logits == m1, lane, LANES), axis=-1, keepdims=True)
    rest_l = jnp.where(lane == i1, neg, logits)
    m2 = jnp.max(rest_l, axis=-1, keepdims=True)
    i2 = jnp.min(jnp.where(rest_l == m2, lane, LANES), axis=-1, keepdims=True)
    e2 = jnp.exp(m2 - m1)
    w1 = 1.0 / (1.0 + e2)
    w2 = e2 / (1.0 + e2)
    hot1 = lane == i1
    hot2 = lane == i2
    tot = jnp.where(jnp.logical_or(hot1, hot2), 1.0, 0.0)
    trow = lax.broadcasted_iota(_I32, (TM, TM), 0)
    tcol = lax.broadcasted_iota(_I32, (TM, TM), 1)
    before = _dot((trow > tcol).astype(_BF), tot.astype(_BF))
    cnt = jnp.sum(tot, axis=0, keepdims=True)
    cnt_pad = jnp.ceil(cnt * (1.0 / RUN_ALIGN)) * RUN_ALIGN
    erow = lax.broadcasted_iota(_I32, (LANES, LANES), 0)
    ecol = lax.broadcasted_iota(_I32, (LANES, LANES), 1)
    start = _dot(jnp.broadcast_to(cnt_pad, (SUBLANES, LANES)).astype(_BF),
                 (erow < ecol).astype(_BF))[0:1, :]
    slot = before + start
    pos1 = jnp.sum(jnp.where(hot1, slot, 0.0), axis=-1, keepdims=True)
    pos2 = jnp.sum(jnp.where(hot2, slot, 0.0), axis=-1, keepdims=True)
    route = jnp.where(lane == 0, pos1, jnp.where(lane == 1, pos2,
                      jnp.where(lane == 2, w1, jnp.where(lane == 3, w2, 0.0))))
    route_ref[...] = route
    route_t_ref[...] = route.T[0:SUBLANES, :]
    cnt_ref[...] = jnp.broadcast_to(cnt, (SUBLANES, LANES)).astype(_I32)


def _mix_kernel(with_router, split_x, u_ref, v_ref, sga_ref, sgb_ref, op_ref, os_ref, *rest):
    if split_x:
        xp_ref, xs_ref, *rest = rest
        x = jnp.where(_is_sample_tile(), xs_ref[...], xp_ref[...])
    else:
        x_ref, *rest = rest
        x = x_ref[...]
    (mp_g1, ms_g1, mp_sh, ms_sh, mp_sc, ms_sc,
     wmix_ref, bmix_ref, wa_ref, wb_ref, wo_ref, npost_ref, npre_ref, *rest) = rest
    if with_router:
        wr_hi_ref, wr_lo_ref, br_ref, x1_ref, h2_ref, route_ref, route_t_ref, cnt_ref, a_scr = rest
    else:
        x1_ref, h2_ref, a_scr = rest
    for c in range(TM // GM_CHUNK):
        rows = slice(c * GM_CHUNK, (c + 1) * GM_CHUNK)
        for gi in range(GM_GROUPS):
            cols = slice(gi * GM_GROUP_DIM, (gi + 1) * GM_GROUP_DIM)
            mixed = _dot(wmix_ref[gi], v_ref[rows, cols])
            bias = bmix_ref[gi]
            mixed = mixed + jnp.concatenate([bias] * (GM_GROUP_DIM // LANES), axis=1)
            a_scr[rows, cols] = (u_ref[rows, cols].astype(_F32) * mixed).astype(_BF)
    br_a = _dot(a_scr[...], wa_ref[...])
    o = jnp.where(_is_sample_tile(), os_ref[...], op_ref[...])
    br_b = _dot(o, wb_ref[...])
    merged = sga_ref[...].astype(_F32) * br_a + sgb_ref[...].astype(_F32) * br_b
    mix = _dot(merged.astype(_BF), wo_ref[...])
    g1 = _mod_rows(mp_g1, ms_g1)
    nm = _rms(mix, npost_ref[...])
    x1 = x + _per_seq(nm, lambda t: t * g1[:, None, :])
    x1_ref[...] = x1
    h2 = _modulated_norm(x1, npre_ref, mp_sh, ms_sh, mp_sc, ms_sc)
    h2_bf = h2.astype(_BF)
    h2_ref[...] = h2_bf
    if with_router:
        _route(h2, h2_bf, wr_hi_ref, wr_lo_ref, br_ref, route_ref, route_t_ref, cnt_ref)


def _mix_call(layer, moe_idx, u, v, sga, sgb, o_p, o_s, x, modp, mods, wmix, bmix,
              wa, wb, wo, npost, npre, wr_hi=None, wr_lo=None, br=None):
    with_router = moe_idx is not None
    split_x = isinstance(x, tuple)
    ty = lambda i: (i >= NT_P).astype(_I32)
    wspec = pl.BlockSpec((None, D_MODEL, D_MODEL), lambda i: (layer, 0, 0))
    in_specs = [_tile_spec(), _tile_spec(), _tile_spec(), _tile_spec()] + _split_specs(HG_WIDTH)
    in_specs += _split_specs() if split_x else [_tile_spec()]
    in_specs += _mod_specs(layer, 2) + _mod_specs(layer, 3) + _mod_specs(layer, 4)
    in_specs += [pl.BlockSpec((None, None, GM_GROUPS, GM_CHUNK, GM_CHUNK), lambda i: (layer, ty(i), 0, 0, 0)),
                 pl.BlockSpec((None, None, GM_GROUPS, GM_CHUNK, LANES), lambda i: (layer, ty(i), 0, 0, 0)),
                 wspec, wspec, wspec, _row_spec(layer), _row_spec(layer)]
    args = [u, v, sga, sgb, o_p, o_s] + (list(x) if split_x else [x])
    args += [modp, mods, modp, mods, modp, mods, wmix, bmix, wa, wb, wo, npost, npre]
    out_specs = [_tile_spec(), _tile_spec()]
    out_shape = [jax.ShapeDtypeStruct((N_TOK, D_MODEL), _F32),
                 jax.ShapeDtypeStruct((N_TOK, D_MODEL), _BF)]
    if with_router:
        rspec = pl.BlockSpec((None, D_MODEL, LANES), lambda i: (moe_idx, 0, 0))
        in_specs += [rspec, rspec, pl.BlockSpec((None, 1, LANES), lambda i: (moe_idx, 0, 0))]
        args += [wr_hi, wr_lo, br]
        out_specs += [_tile_spec(LANES),
                      pl.BlockSpec((None, SUBLANES, TM), lambda i: (i, 0, 0)),
                      pl.BlockSpec((None, SUBLANES, LANES), lambda i: (i, 0, 0))]
        out_shape += [jax.ShapeDtypeStruct((N_TOK, LANES), _F32),
                      jax.ShapeDtypeStruct((NT, SUBLANES, TM), _F32),
                      jax.ShapeDtypeStruct((NT, SUBLANES, LANES), _I32)]
    return pl.pallas_call(
        functools.partial(_mix_kernel, with_router, split_x),
        grid=(NT,),
        in_specs=in_specs,
        out_specs=out_specs,
        out_shape=out_shape,
        scratch_shapes=[pltpu.VMEM((TM, GM_WIDTH), _BF)],
        compiler_params=_cparams(1),
        name="mix",
    )(*args)


def _swiglu(h, wg_ref, wu_ref, wd_ref):
    acc = None
    for k in range(D_FF // FF_CHUNK):
        cs = slice(k * FF_CHUNK, (k + 1) * FF_CHUNK)
        gate = _dot(h, wg_ref[:, cs])
        up = _dot(h, wu_ref[:, cs])
        act = (_silu(gate) * up).astype(_BF)
        part = _dot(act, wd_ref[cs, :])
        acc = part if acc is None else acc + part
    return acc


def _layer_out(last, x1_ref, f, mp_g2, ms_g2, npost_ref, tail_refs):
    g2 = _mod_rows(mp_g2, ms_g2)
    nf = _rms(f, npost_ref[...])
    x2 = x1_ref[...] + _per_seq(nf, lambda t: t * g2[:, None, :])
    if last:
        yp_ref, ys_ref = tail_refs

        @pl.when(_is_sample_tile())
        def _():
            ys_ref[...] = x2

        @pl.when(jnp.logical_not(_is_sample_tile()))
        def _():
            yp_ref[...] = x2
    else:
        mp_sh, ms_sh, mp_sc, ms_sc, nw_ref, x_ref, h_ref = tail_refs
        x_ref[...] = x2
        h_ref[...] = _modulated_norm(x2, nw_ref, mp_sh, ms_sh, mp_sc, ms_sc).astype(_BF)


def _layer_out_specs(layer, last, modp, mods, npre_mix):
    if last:
        return [], [], _split_specs(), [jax.ShapeDtypeStruct((N_PROMPT, D_MODEL), _F32),
                                        jax.ShapeDtypeStruct((N_SAMPLE, D_MODEL), _F32)]
    in_specs = _mod_specs(layer + 1, 0) + _mod_specs(layer + 1, 1) + [_row_spec(layer + 1)]
    return (in_specs, [modp, mods, modp, mods, npre_mix], [_tile_spec(), _tile_spec()],
            [jax.ShapeDtypeStruct((N_TOK, D_MODEL), _F32), jax.ShapeDtypeStruct((N_TOK, D_MODEL), _BF)])


def _ffn_kernel(last, h_ref, x1_ref, mp_g2, ms_g2, wg_ref, wu_ref, wd_ref, npost_ref, *tail_refs):
    f = _swiglu(h_ref[...], wg_ref, wu_ref, wd_ref)
    _layer_out(last, x1_ref, f, mp_g2, ms_g2, npost_ref, tail_refs)


def _ffn_call(layer, j, h2, x1, modp, mods, wg, wu, wd, npost, npre_mix):
    last = layer == DEPTH - 1
    extra_specs, extra_args, out_specs, out_shape = _layer_out_specs(layer, last, modp, mods, npre_mix)
    w_in = pl.BlockSpec((None, D_MODEL, D_FF), lambda i: (j, 0, 0), pipeline_mode=pl.Buffered(1))
    w_out = pl.BlockSpec((None, D_FF, D_MODEL), lambda i: (j, 0, 0), pipeline_mode=pl.Buffered(1))
    return pl.pallas_call(
        functools.partial(_ffn_kernel, last),
        grid=(NT,),
        in_specs=[_tile_spec(), _tile_spec()] + _mod_specs(layer, 5) + [
            w_in, w_in, w_out, _row_spec(layer)] + extra_specs,
        out_specs=out_specs,
        out_shape=out_shape,
        compiler_params=_cparams(1),
        name="ffn",
    )(h2, x1, modp, mods, wg, wu, wd, npost, *extra_args)


def _route_tables(cnt):
    cnt = cnt[:, 0, :N_EXPERTS]
    n = (cnt + (RUN_ALIGN - 1)) // RUN_ALIGN * RUN_ALIGN
    s = jnp.cumsum(n, axis=1) - n
    rows_e = jnp.sum(n, axis=0)
    region = (rows_e + (TG - 1)) // TG * TG
    region_end = jnp.cumsum(region)
    off = region_end - region
    p = off[None, :] + jnp.cumsum(n, axis=0) - n
    n_tiles = region_end[-1] // TG
    tile_row0 = jnp.minimum(jnp.arange(R_TILES, dtype=_I32), n_tiles - 1) * TG
    tile_expert = jnp.sum((tile_row0[:, None] >= region_end[None, :]).astype(_I32), axis=1)
    flat = lambda a: a.reshape(-1).astype(_I32)
    return dict(p=flat(p), s=flat(s), n=flat(n), tail_p=flat(off + rows_e), tail_n=flat(region - rows_e),
                tile_expert=flat(tile_expert), n_tiles=flat(n_tiles))


def _for_each_run_piece(n, src0, dst0, fn):
    for size in RUN_SIZES:
        done = n & (-2 * size)

        @pl.when((n & size) != 0)
        def _():
            fn(pl.multiple_of(src0 + done, RUN_ALIGN), pl.multiple_of(dst0 + done, RUN_ALIGN), size)


def _dispatch_kernel(p_tab, s_tab, n_tab, tail_p, tail_n, nt_ref, h_ref, rt_ref, xs_ref,
                     comp_scr, zero_scr, sem):
    t = pl.program_id(0)
    pos1 = rt_ref[0:1, :].astype(_I32)
    pos2 = rt_ref[1:2, :].astype(_I32)
    r = lax.broadcasted_iota(_I32, (CB, TM), 0)
    perm = jnp.where(jnp.logical_or(r == pos1, r == pos2), 1.0, 0.0).astype(_BF)
    comp_scr[...] = _dot(perm, h_ref[...]).astype(_BF)

    def runs(go):
        for e in range(N_EXPERTS):
            idx = t * N_EXPERTS + e

            def piece(src, dst, size):
                go(pltpu.make_async_copy(comp_scr.at[pl.ds(src, size)], xs_ref.at[pl.ds(dst, size)], sem))

            _for_each_run_piece(n_tab[idx], s_tab[idx], p_tab[idx], piece)

    runs(lambda cp: cp.start())
    runs(lambda cp: cp.wait())

    @pl.when(t == NT - 1)
    def _():
        zero_scr[...] = jnp.zeros_like(zero_scr)

        def tails(go):
            for e in range(N_EXPERTS):
                def piece(src, dst, size):
                    go(pltpu.make_async_copy(zero_scr.at[pl.ds(0, size)], xs_ref.at[pl.ds(dst, size)], sem))

                _for_each_run_piece(tail_n[e], 0, tail_p[e], piece)

        tails(lambda cp: cp.start())
        tails(lambda cp: cp.wait())

        def spare_tile(go):
            def body(r, carry):
                dst = pl.multiple_of(r * TG, TG)
                go(pltpu.make_async_copy(zero_scr, xs_ref.at[pl.ds(dst, TG)], sem))
                return carry
            lax.fori_loop(nt_ref[0], R_TILES, body, 0)

        spare_tile(lambda cp: cp.start())
        spare_tile(lambda cp: cp.wait())


def _dispatch_call(tabs, h2, route_t):
    grid_spec = pltpu.PrefetchScalarGridSpec(
        num_scalar_prefetch=6,
        grid=(NT,),
        in_specs=[_tile_spec(), pl.BlockSpec((None, SUBLANES, TM), lambda i, *_: (i, 0, 0))],
        out_specs=pl.BlockSpec(memory_space=pl.ANY),
        scratch_shapes=[pltpu.VMEM((CB, D_MODEL), _BF), pltpu.VMEM((TG, D_MODEL), _BF),
                        pltpu.SemaphoreType.DMA(())])
    return pl.pallas_call(
        _dispatch_kernel,
        grid_spec=grid_spec,
        out_shape=jax.ShapeDtypeStruct((N_ROWS, D_MODEL), _BF),
        compiler_params=_cparams(1),
        name="dispatch",
    )(tabs["p"], tabs["s"], tabs["n"], tabs["tail_p"], tabs["tail_n"], tabs["n_tiles"], h2, route_t)


def _expert_kernel(te_ref, nt_ref, x_ref, wg_ref, wu_ref, wd_ref, y_ref):
    r = pl.program_id(0)

    @pl.when(r < nt_ref[0])
    def _():
        y_ref[...] = _swiglu(x_ref[...], wg_ref, wu_ref, wd_ref).astype(_BF)

    @pl.when(r >= nt_ref[0])
    def _():
        y_ref[...] = jnp.zeros_like(y_ref)


def _expert_call(j, tabs, xs, wg, wu, wd):
    w_in = pl.BlockSpec((None, None, D_MODEL, D_FF), lambda r, te, nt: (j, te[r], 0, 0))
    w_out = pl.BlockSpec((None, None, D_FF, D_MODEL), lambda r, te, nt: (j, te[r], 0, 0))
    grid_spec = pltpu.PrefetchScalarGridSpec(
        num_scalar_prefetch=2,
        grid=(R_TILES,),
        in_specs=[pl.BlockSpec((TG, D_MODEL), lambda r, te, nt: (jnp.minimum(r, nt[0] - 1), 0)),
                  w_in, w_in, w_out],
        out_specs=pl.BlockSpec((TG, D_MODEL), lambda r, te, nt: (r, 0)))
    return pl.pallas_call(
        _expert_kernel,
        grid_spec=grid_spec,
        out_shape=jax.ShapeDtypeStruct((N_ROWS, D_MODEL), _BF),
        compiler_params=_cparams(1),
        name="experts",
    )(tabs["tile_expert"], tabs["n_tiles"], xs, wg, wu, wd)


def _combine_kernel(last, p_tab, s_tab, n_tab, route_ref, x1_ref, mp_g2, ms_g2, npost_ref, ys_ref, *rest):
    *tail_refs, yc_scr, sem = rest
    t = pl.program_id(0)
    yc_scr[...] = jnp.zeros_like(yc_scr)

    def runs(go):
        for e in range(N_EXPERTS):
            idx = t * N_EXPERTS + e

            def piece(src, dst, size):
                go(pltpu.make_async_copy(ys_ref.at[pl.ds(dst, size)], yc_scr.at[pl.ds(src, size)], sem))

            _for_each_run_piece(n_tab[idx], s_tab[idx], p_tab[idx], piece)

    runs(lambda cp: cp.start())
    runs(lambda cp: cp.wait())
    route = route_ref[...]
    pos1 = route[:, 0:1].astype(_I32)
    pos2 = route[:, 1:2].astype(_I32)
    w1 = route[:, 2:3]
    w2 = route[:, 3:4]
    slot = lax.broadcasted_iota(_I32, (TM, CB), 1)
    yc = yc_scr[...]
    pick1 = jnp.where(slot == pos1, 1.0, 0.0).astype(_BF)
    pick2 = jnp.where(slot == pos2, 1.0, 0.0).astype(_BF)
    f = w1 * _dot(pick1, yc) + w2 * _dot(pick2, yc)
    _layer_out(last, x1_ref, f, mp_g2, ms_g2, npost_ref, tail_refs)


def _combine_call(layer, tabs, route, x1, modp, mods, npost, ys, npre_mix):
    last = layer == DEPTH - 1
    extra_specs, extra_args, out_specs, out_shape = _layer_out_specs(layer, last, modp, mods, npre_mix)
    grid_spec = pltpu.PrefetchScalarGridSpec(
        num_scalar_prefetch=3,
        grid=(NT,),
        in_specs=[_tile_spec(LANES), _tile_spec()] + _mod_specs(layer, 5) + [
            _row_spec(layer), pl.BlockSpec(memory_space=pl.ANY)] + extra_specs,
        out_specs=out_specs,
        scratch_shapes=[pltpu.VMEM((CB, D_MODEL), _BF), pltpu.SemaphoreType.DMA(())])
    return pl.pallas_call(
        functools.partial(_combine_kernel, last),
        grid_spec=grid_spec,
        out_shape=out_shape,
        compiler_params=_cparams(1),
        name="combine",
    )(tabs["p"], tabs["s"], tabs["n"], route, x1, modp, mods, npost, ys, *extra_args)


def _spatial_tables(gm_ws, gm_bs):
    mask_p = jnp.tril(jnp.ones((GM_CHUNK, GM_CHUNK), bool))
    w_p = jnp.where(mask_p, gm_ws, 0.0)
    mask_s = jnp.tril(jnp.ones((DEC_SEQ, DEC_SEQ), bool))
    w_small = jnp.where(mask_s, gm_ws[:, :, :DEC_SEQ, :DEC_SEQ], 0.0)
    eye = jnp.eye(GM_CHUNK // DEC_SEQ, dtype=gm_ws.dtype)
    w_s = jnp.einsum("ab,lgts->lgatbs", eye, w_small).reshape(gm_ws.shape)
    b_p = gm_bs
    b_s = jnp.tile(gm_bs[:, :, :DEC_SEQ], (1, 1, GM_CHUNK // DEC_SEQ))
    wmix = jnp.stack([w_p, w_s], axis=1).astype(_BF)
    bmix = jnp.stack([b_p, b_s], axis=1)[..., None]
    bmix = jnp.broadcast_to(bmix, bmix.shape[:-1] + (LANES,)).astype(_F32)
    return wmix, bmix


def kernel(x_prompt, x_sample, c_prompt, c_sample, state_hgrn, w_in, gm_ln_w, gm_ln_b, gm_ws, gm_bs,
           hg_lb_raw, hg_norm_w, w_branch_a, w_branch_b, w_out, w_ada, b_ada,
           norm_pre_mix, norm_post_mix, norm_pre_ffn, norm_post_ffn,
           w_ffn_gate, w_ffn_up, w_ffn_down, w_router, b_router, w_exp_gate, w_exp_up, w_exp_down):
    x = (x_prompt.reshape(N_PROMPT, D_MODEL), x_sample.reshape(N_SAMPLE, D_MODEL))
    c_all = jnp.concatenate([c_prompt, c_sample], axis=0)
    mod = _ada_call(c_all, w_ada, b_ada)
    modp = mod[:, :BATCH].reshape(DEPTH, BATCH, 1, 6 * D_MODEL)
    mods = mod[:, BATCH:]
    loglb = _lb_call(hg_lb_raw)
    aux = jnp.zeros((DEPTH, SUBLANES, D_MODEL), _F32)
    aux = aux.at[:, 0].set(gm_ln_w).at[:, 1].set(gm_ln_b).at[:, 2].set(loglb)
    wmix, bmix = _spatial_tables(gm_ws, gm_bs)
    row3 = lambda a: a.reshape(a.shape[0], 1, a.shape[1])
    w_in_bf = w_in.astype(_BF)
    wa_bf, wb_bf, wo_bf = w_branch_a.astype(_BF), w_branch_b.astype(_BF), w_out.astype(_BF)
    wfg, wfu, wfd = w_ffn_gate.astype(_BF), w_ffn_up.astype(_BF), w_ffn_down.astype(_BF)
    weg, weu, wed = w_exp_gate.astype(_BF), w_exp_up.astype(_BF), w_exp_down.astype(_BF)
    wr_pad = jnp.pad(w_router, ((0, 0), (0, 0), (0, LANES - N_EXPERTS)))
    wr_hi = wr_pad.astype(_BF)
    wr_lo = (wr_pad - wr_hi.astype(_F32)).astype(_BF)
    br_pad = jnp.pad(b_router, ((0, 0), (0, LANES - N_EXPERTS))).reshape(-1, 1, LANES)
    npre_mix, npost_mix = row3(norm_pre_mix), row3(norm_post_mix)
    npre_ffn, npost_ffn = row3(norm_pre_ffn), row3(norm_post_ffn)
    hg_nw = row3(hg_norm_w)

    hp, vs = [], []
    sample_states = jnp.zeros(state_hgrn.shape, _F32)
    h = _prenorm_call(0, x[0], x[1], modp, mods, npre_mix)
    for l in range(DEPTH):
        u, v, v32, q, g = _inproj_call(l, 0, ("gelu", "gelu_ln", "silu", "logf"), h, w_in_bf, aux)
        iv, sg, sga, sgb = _inproj_call(l, 4, ("id", "silu", "sigmoid", "sigmoid"), h, w_in_bf, aux)
        o_p, s_p = _hgrn_prompt_call(l, q, g, iv, sg, hg_nw)
        o_s, sample_states = _hgrn_sample_call(l, q, g, iv, sg, hg_nw, state_hgrn, sample_states)
        j = l // 2
        if l % 2 == 0:
            x1, h2 = _mix_call(l, None, u, v, sga, sgb, o_p, o_s, x, modp, mods, wmix, bmix,
                               wa_bf, wb_bf, wo_bf, npost_mix, npre_ffn)
            out = _ffn_call(l, j, h2, x1, modp, mods, wfg, wfu, wfd, npost_ffn, npre_mix)
        else:
            x1, h2, route, route_t, cnt = _mix_call(l, j, u, v, sga, sgb, o_p, o_s, x, modp, mods, wmix, bmix,
                                                    wa_bf, wb_bf, wo_bf, npost_mix, npre_ffn,
                                                    wr_hi, wr_lo, br_pad)
            tabs = _route_tables(cnt)
            xs = _dispatch_call(tabs, h2, route_t)
            ys = _expert_call(j, tabs, xs, weg, weu, wed)
            out = _combine_call(l, tabs, route, x1, modp, mods, npost_ffn, ys, npre_mix)
        if l < DEPTH - 1:
            x, h = out
        hp.append(s_p)
        vs.append(v32.reshape(DEC_BATCH, DEC_SEQ, GM_WIDTH))
    y_prompt = out[0].reshape(BATCH, SEQ, D_MODEL)
    y_sample = out[1].reshape(DEC_BATCH, DEC_SEQ, D_MODEL)
    return (y_prompt, y_sample, jnp.stack(hp, axis=0), sample_states, jnp.stack(vs, axis=0))
```

```python
import functools
import math

import jax
import jax.numpy as jnp
from jax import lax
from jax.experimental import pallas as pl
from jax.experimental.pallas import tpu as pltpu

D_MODEL = 1024
BATCH = 8
SEQ = 2048
DEPTH = 4
DEC_BATCH = 128
DEC_SEQ = 8
GM_WIDTH = 1024
GM_GROUPS = 4
GM_GROUP_DIM = GM_WIDTH // GM_GROUPS
GM_CHUNK = 128
HG_HEADS = 8
HG_DK = 128
HG_DV = 128
HG_WIDTH = HG_HEADS * HG_DK
D_FF = 2816
N_EXPERTS = 8
TOP_K = 2
IN_COLS = 2 * GM_WIDTH + 4 * HG_WIDTH + 2 * D_MODEL
EPS = 1e-6

LANES = 128
SUBLANES = 8
BF16_ROWS = 16
TM = 512
N_PROMPT = BATCH * SEQ
N_SAMPLE = DEC_BATCH * DEC_SEQ
N_TOK = N_PROMPT + N_SAMPLE
NT_P = N_PROMPT // TM
NT_S = N_SAMPLE // TM
NT = NT_P + NT_S
TILES_PER_SEQ = SEQ // TM
SEQ_PER_TILE = TM // DEC_SEQ
HG_C = 64
HG_BLK = 256
HG_SB = 8
FF_CHUNK = D_FF // 2
INPROJ_ROWS = 256
RIDER_BLOCKS = 32
EXP_RANGE = 70.0
VMEM_LIMIT = 56 * 1024 * 1024

TG = 512
RUN_ALIGN = BF16_ROWS
RUN_SIZES = tuple(TM >> k for k in range((TM // RUN_ALIGN).bit_length()))
CB = TOP_K * TM + N_EXPERTS * RUN_ALIGN
_ROWS_BOUND = TOP_K * N_TOK + NT * N_EXPERTS * (RUN_ALIGN - 1) + N_EXPERTS * (TG - RUN_ALIGN)
R_TILES = -(-_ROWS_BOUND // TG)
N_ROWS = R_TILES * TG

_BF = jnp.bfloat16
_F32 = jnp.float32
_I32 = jnp.int32


def _cparams(n_axes):
    return pltpu.CompilerParams(dimension_semantics=("arbitrary",) * n_axes,
                                vmem_limit_bytes=VMEM_LIMIT)


def _dot(a, b):
    return jnp.dot(a, b, preferred_element_type=_F32)


def _sigmoid(x):
    return 0.5 + 0.5 * jnp.tanh(0.5 * x)


def _silu(x):
    hx = 0.5 * x
    return hx + hx * jnp.tanh(hx)


def _gelu(x):
    return 0.5 * x * (1.0 + lax.erf(x * (1.0 / math.sqrt(2.0))))


def _rms(x, w):
    return x * lax.rsqrt(jnp.mean(x * x, axis=-1, keepdims=True) + EPS) * w


def _is_sample_tile():
    return pl.program_id(0) >= NT_P


def _mod_rows(mp_ref, ms_ref):
    mp = jnp.broadcast_to(mp_ref[...], ms_ref.shape)
    return jnp.where(_is_sample_tile(), ms_ref[...], mp)


def _per_seq(x, fn_rows):
    x3 = x.reshape(SEQ_PER_TILE, DEC_SEQ, x.shape[-1])
    return fn_rows(x3).reshape(TM, x.shape[-1])


def _mod_specs(layer, col_block):
    mp = pl.BlockSpec((None, None, 1, D_MODEL),
                      lambda i, *_: (layer, jnp.minimum(i // TILES_PER_SEQ, BATCH - 1), 0, col_block))
    ms = pl.BlockSpec((None, SEQ_PER_TILE, D_MODEL),
                      lambda i, *_: (layer, jnp.maximum(i - NT_P, 0), col_block))
    return [mp, ms]


def _tile_spec(width=D_MODEL):
    return pl.BlockSpec((TM, width), lambda i, *_: (i, 0))


def _row_spec(layer, width=D_MODEL):
    return pl.BlockSpec((None, 1, width), lambda i, *_: (layer, 0, 0))


def _ada_kernel(c_ref, w_ref, b_ref, o_ref):
    s = _silu(c_ref[...]).astype(_BF)
    o_ref[...] = _dot(s, w_ref[...].astype(_BF)) + b_ref[...]


def _ada_call(c_all, w_ada, b_ada):
    nb = 4
    wcol = 6 * D_MODEL // nb
    rows = c_all.shape[0]
    return pl.pallas_call(
        _ada_kernel,
        grid=(DEPTH, nb),
        in_specs=[pl.BlockSpec((rows, D_MODEL), lambda l, j: (0, 0)),
                  pl.BlockSpec((None, D_MODEL, wcol), lambda l, j: (l, 0, j)),
                  pl.BlockSpec((None, 1, wcol), lambda l, j: (l, 0, j))],
        out_specs=pl.BlockSpec((None, rows, wcol), lambda l, j: (l, 0, j)),
        out_shape=jax.ShapeDtypeStruct((DEPTH, rows, 6 * D_MODEL), _F32),
        compiler_params=_cparams(2),
        name="ada",
    )(c_all, w_ada, b_ada.reshape(DEPTH, 1, 6 * D_MODEL))


def _lb_kernel(raw_ref, loglb_ref):
    raw = raw_ref[...]
    m = jnp.max(raw, axis=0, keepdims=True)
    e = jnp.exp(raw - m)
    p = e / jnp.sum(e, axis=0, keepdims=True)
    acc = jnp.zeros((1, HG_WIDTH), _F32)
    for l in range(DEPTH):
        if l > 0:
            acc = acc + p[l:l + 1, :]
        loglb_ref[l:l + 1, :] = jnp.log(acc)


def _lb_call(raw):
    return pl.pallas_call(
        _lb_kernel,
        out_shape=jax.ShapeDtypeStruct((DEPTH, HG_WIDTH), _F32),
        name="lower_bounds",
    )(raw)


def _modulated_norm(x, nw_ref, mp_sh, ms_sh, mp_sc, ms_sc):
    xn = _rms(x, nw_ref[...])
    sh = _mod_rows(mp_sh, ms_sh)
    sc = _mod_rows(mp_sc, ms_sc)
    return _per_seq(xn, lambda x3: x3 * (1.0 + sc)[:, None, :] + sh[:, None, :])


def _prenorm_kernel(xp_ref, xs_ref, mp_sh, ms_sh, mp_sc, ms_sc, nw_ref, h_ref):
    x = jnp.where(_is_sample_tile(), xs_ref[...], xp_ref[...])
    h_ref[...] = _modulated_norm(x, nw_ref, mp_sh, ms_sh, mp_sc, ms_sc).astype(_BF)


def _split_specs(width=D_MODEL):
    return [pl.BlockSpec((TM, width), lambda i, *_: (jnp.minimum(i, NT_P - 1), 0)),
            pl.BlockSpec((TM, width), lambda i, *_: (jnp.maximum(i - NT_P, 0), 0))]


def _prenorm_call(layer, x_p, x_s, modp, mods, norm_w):
    return pl.pallas_call(
        _prenorm_kernel,
        grid=(NT,),
        in_specs=_split_specs() + _mod_specs(layer, 0) + _mod_specs(layer, 1) + [_row_spec(layer)],
        out_specs=_tile_spec(),
        out_shape=jax.ShapeDtypeStruct((N_TOK, D_MODEL), _BF),
        compiler_params=_cparams(1),
        name="prenorm",
    )(x_p, x_s, modp, mods, modp, mods, norm_w)


def _with_cast_rider(body, n_in):
    def wrapped(*refs):
        rider_in, rider_out = refs[n_in], refs[-1]
        rider_out[...] = rider_in[...].astype(_BF)
        body(*refs[:n_in], *refs[n_in + 1:-1])
    return wrapped


def _add_rider(rider, body, n_in, in_specs, args, out_specs, out_shape):
    if rider is None:
        return body
    w, j = rider
    w2 = w.reshape(-1, w.shape[-1])
    rows_per_layer = w2.shape[0] // w.shape[0]
    rows = rows_per_layer // RIDER_BLOCKS
    blk = lambda i: jnp.minimum(i, RIDER_BLOCKS - 1)
    in_specs.append(pl.BlockSpec((rows, w2.shape[1]), lambda i, *_: (j * RIDER_BLOCKS + blk(i), 0)))
    args.append(w2)
    out_specs.append(pl.BlockSpec((rows, w2.shape[1]), lambda i, *_: (blk(i), 0)))
    out_shape.append(jax.ShapeDtypeStruct((rows_per_layer, w2.shape[1]), _BF))
    return _with_cast_rider(body, n_in)


def _inproj_kernel(kinds, row_block, h_ref, w_ref, aux_ref, *out_refs):
    for rb in range(TM // row_block):
        rs = slice(rb * row_block, (rb + 1) * row_block)
        h = h_ref[rs, :]
        outs = list(out_refs)
        for j, kind in enumerate(kinds):
            z = _dot(h, w_ref[:, j * D_MODEL:(j + 1) * D_MODEL])
            if kind == "gelu":
                outs.pop(0)[rs, :] = _gelu(z).astype(_BF)
            elif kind == "gelu_ln":
                a = _gelu(z)
                mu = jnp.mean(a, axis=-1, keepdims=True)
                ac = a - mu
                vn = ac * lax.rsqrt(jnp.mean(ac * ac, axis=-1, keepdims=True) + EPS)
                vn = vn * aux_ref[0:1, :] + aux_ref[1:2, :]
                outs.pop(0)[rs, :] = vn.astype(_BF)
                v32_ref = outs.pop(0)

                @pl.when(_is_sample_tile())
                def _():
                    v32_ref[rs, :] = vn
            elif kind == "silu":
                outs.pop(0)[rs, :] = _silu(z).astype(_BF)
            elif kind == "sigmoid":
                outs.pop(0)[rs, :] = _sigmoid(z).astype(_BF)
            elif kind == "id":
                outs.pop(0)[rs, :] = z.astype(_BF)
            elif kind == "logf":
                loglb = aux_ref[2:3, :]
                tail = jnp.log(1.0 + jnp.exp(-jnp.abs(z)))
                a = jnp.minimum(z, 0.0) - tail
                c = loglb - jnp.maximum(z, 0.0) - tail
                outs.pop(0)[rs, :] = jnp.maximum(a, c) + jnp.log(1.0 + jnp.exp(-jnp.abs(z - loglb)))
            else:
                raise ValueError(kind)


def _inproj_call(layer, sec0, kinds, h, w_in_bf, aux, row_block, rider=None):
    nsec = len(kinds)
    out_specs, out_shape = [], []
    for kind in kinds:
        dt = _F32 if kind == "logf" else _BF
        out_specs.append(_tile_spec())
        out_shape.append(jax.ShapeDtypeStruct((N_TOK, D_MODEL), dt))
        if kind == "gelu_ln":
            out_specs.append(pl.BlockSpec((TM, D_MODEL), lambda i: (jnp.maximum(i - NT_P, 0), 0)))
            out_shape.append(jax.ShapeDtypeStruct((N_SAMPLE, D_MODEL), _F32))
    in_specs = [_tile_spec(),
                pl.BlockSpec((None, D_MODEL, nsec * D_MODEL), lambda i: (layer, 0, sec0 // nsec)),
                pl.BlockSpec((None, SUBLANES, D_MODEL), lambda i: (layer, 0, 0))]
    args = [h, w_in_bf, aux]
    body = _add_rider(rider, functools.partial(_inproj_kernel, kinds, row_block), len(args),
                      in_specs, args, out_specs, out_shape)
    return pl.pallas_call(
        body,
        grid=(NT,),
        in_specs=in_specs,
        out_specs=out_specs,
        out_shape=out_shape,
        compiler_params=_cparams(1),
        name=f"inproj_{sec0}",
    )(*args)


def _col_bcast(row):
    n = row.shape[-1]
    return jnp.broadcast_to(row, (n, n)).T


def _head_out(o, nw, sg):
    return (_rms(o, nw) * sg.astype(_F32)).astype(_BF)


def _hgrn_exact(ns, rows, q_ref, g_ref, i_ref, sg_ref, nw_ref, o_ref, s_get, s_put):
    n = ns * DEC_SEQ
    shape3 = (ns, DEC_SEQ, HG_WIDTH)
    t_row = jnp.bitwise_and(lax.broadcasted_iota(_I32, (n, HG_WIDTH), 0), DEC_SEQ - 1)
    t3 = lax.broadcasted_iota(_I32, (ns, DEC_SEQ, 1), 1)
    g = g_ref[rows, :]
    b = g
    shift = 1
    while shift < DEC_SEQ:
        b = b + jnp.where(t_row >= shift, pltpu.roll(b, shift, axis=0), 0.0)
        shift *= 2
    kk = 1.0 - jnp.exp(g)
    q = q_ref[rows, :].astype(_F32)
    b3 = b.reshape(shape3)
    bend3 = b3[:, DEC_SEQ - 1:DEC_SEQ, :]
    qe = (q * jnp.exp(b)).astype(_BF)
    kdec = (kk.reshape(shape3) * jnp.exp(bend3 - b3)).reshape(n, HG_WIDTH).astype(_BF)
    e_end = jnp.exp(bend3)
    iv_all = i_ref[rows, :]
    sg_all = sg_ref[rows, :]
    q3, k3, i3 = q.reshape(shape3), kk.reshape(shape3), iv_all.astype(_F32).reshape(shape3)
    for h in range(HG_HEADS):
        hs = slice(h * HG_DK, (h + 1) * HG_DK)
        qh, bh, kh, ih = q3[:, :, hs], b3[:, :, hs], k3[:, :, hs], i3[:, :, hs]
        intra = jnp.zeros((ns, DEC_SEQ, HG_DV), _F32)
        for j in range(DEC_SEQ):
            decay = jnp.exp(jnp.minimum(bh - bh[:, j:j + 1, :], 0.0))
            w = jnp.sum(qh * decay * kh[:, j:j + 1, :], axis=-1, keepdims=True)
            intra = intra + jnp.where(t3 >= j, w, 0.0) * ih[:, j:j + 1, :]
        outs = []
        for si in range(ns):
            rs = slice(si * DEC_SEQ, (si + 1) * DEC_SEQ)
            s_old = s_get(si, h)
            outs.append(intra[si] + _dot(qe[rs, hs], s_old.astype(_BF)))
            upd = lax.dot_general(kdec[rs, hs], iv_all[rs, hs], (((0,), (0,)), ((), ())),
                                  preferred_element_type=_F32)
            s_put(si, h, _col_bcast(e_end[si, :, hs]) * s_old + upd)
        o = outs[0] if ns == 1 else jnp.concatenate(outs, axis=0)
        o_ref[rows, hs] = _head_out(o, nw_ref[...], sg_all[:, hs])


def _rows_of_chunks(x, row_in_chunk):
    parts = []
    for c in range(HG_BLK // HG_C):
        r = c * HG_C + row_in_chunk
        parts.append(jnp.broadcast_to(x[r:r + 1, :], (HG_C, x.shape[-1])))
    return jnp.concatenate(parts, axis=0)


def _hgrn_prompt_kernel(q_ref, g_ref, i_ref, sg_ref, nw_ref, o_ref, s_out_ref, s_scr):
    c = pl.program_id(1)

    @pl.when(c == 0)
    def _():
        s_scr[...] = jnp.zeros_like(s_scr)

    g = g_ref[...]
    row = lax.broadcasted_iota(_I32, (HG_BLK, HG_BLK), 0)
    col = lax.broadcasted_iota(_I32, (HG_BLK, HG_BLK), 1)
    tril = jnp.logical_and(row >= col, row // HG_C == col // HG_C).astype(_BF)
    g_hi = g.astype(_BF)
    r1 = g - g_hi.astype(_F32)
    g_mid = r1.astype(_BF)
    g_lo = (r1 - g_mid.astype(_F32)).astype(_BF)
    b = _dot(tril, g_hi) + _dot(tril, g_mid) + _dot(tril, g_lo)
    ref = _rows_of_chunks(b, HG_C // 2 - 1)
    in_range = jnp.max(jnp.abs(b - ref)) <= EXP_RANGE

    @pl.when(in_range)
    def _():
        kk = 1.0 - jnp.exp(g)
        q = q_ref[...].astype(_F32)
        bend = _rows_of_chunks(b, HG_C - 1)
        qt = (q * jnp.exp(b - ref)).astype(_BF)
        kt = (kk * jnp.exp(ref - b)).astype(_BF)
        qe = (q * jnp.exp(b)).astype(_BF)
        kdec = (kk * jnp.exp(bend - b)).astype(_BF)
        e_end = jnp.exp(bend)
        crow = lax.broadcasted_iota(_I32, (HG_C, HG_C), 0)
        ccol = lax.broadcasted_iota(_I32, (HG_C, HG_C), 1)
        causal = crow >= ccol
        for h in range(HG_HEADS):
            hs = slice(h * HG_DK, (h + 1) * HG_DK)
            s_cur = s_scr[h]
            for ci in range(HG_BLK // HG_C):
                rs = slice(ci * HG_C, (ci + 1) * HG_C)
                iv = i_ref[rs, hs]
                scores = lax.dot_general(qt[rs, hs], kt[rs, hs], (((1,), (1,)), ((), ())),
                                         preferred_element_type=_F32)
                scores = jnp.where(causal, scores, 0.0).astype(_BF)
                o = _dot(scores, iv) + _dot(qe[rs, hs], s_cur.astype(_BF))
                upd = lax.dot_general(kdec[rs, hs], iv, (((0,), (0,)), ((), ())),
                                      preferred_element_type=_F32)
                s_cur = _col_bcast(e_end[ci * HG_C:ci * HG_C + 1, hs]) * s_cur + upd
                o_ref[rs, hs] = _head_out(o, nw_ref[...], sg_ref[rs, hs])
            s_scr[h] = s_cur

    @pl.when(jnp.logical_not(in_range))
    def _():
        def s_put(si, h, val):
            s_scr[h] = val

        def sub_chunk(k, carry):
            rows = pl.ds(pl.multiple_of(k * DEC_SEQ, DEC_SEQ), DEC_SEQ)
            _hgrn_exact(1, rows, q_ref, g_ref, i_ref, sg_ref, nw_ref, o_ref, lambda si, h: s_scr[h], s_put)
            return carry

        lax.fori_loop(0, HG_BLK // DEC_SEQ, sub_chunk, 0)

    @pl.when(c == pl.num_programs(1) - 1)
    def _():
        s_out_ref[...] = s_scr[...]


def _hgrn_prompt_call(layer, q, g, iv, sg, hg_norm_w):
    nc = SEQ // HG_BLK
    blk = pl.BlockSpec((HG_BLK, HG_WIDTH), lambda b, c: (b * nc + c, 0))
    return pl.pallas_call(
        _hgrn_prompt_kernel,
        grid=(BATCH, nc),
        in_specs=[blk, blk, blk, blk,
                  pl.BlockSpec((None, 1, HG_DV), lambda b, c: (layer, 0, 0))],
        out_specs=[blk,
                   pl.BlockSpec((None, HG_HEADS, HG_DK, HG_DV), lambda b, c: (b, 0, 0, 0))],
        out_shape=[jax.ShapeDtypeStruct((N_PROMPT, HG_WIDTH), _BF),
                   jax.ShapeDtypeStruct((BATCH, HG_HEADS, HG_DK, HG_DV), _F32)],
        scratch_shapes=[pltpu.VMEM((HG_HEADS, HG_DK, HG_DV), _F32)],
        compiler_params=_cparams(2),
        name="hgrn_prompt",
    )(q, g, iv, sg, hg_norm_w)


def _hgrn_sample_kernel(q_ref, g_ref, i_ref, sg_ref, nw_ref, s_in_ref, all_states_ref, o_ref, s_out_ref):
    del all_states_ref

    def s_put(si, h, val):
        s_out_ref[si, h] = val

    _hgrn_exact(HG_SB, slice(None), q_ref, g_ref, i_ref, sg_ref, nw_ref, o_ref, lambda si, h: s_in_ref[si, h], s_put)


def _hgrn_sample_call(layer, q, g, iv, sg, hg_norm_w, state, all_states):
    rows = HG_SB * DEC_SEQ
    off = N_PROMPT // rows
    blk = pl.BlockSpec((rows, HG_WIDTH), lambda j: (off + j, 0))
    s_blk = pl.BlockSpec((None, HG_SB, HG_HEADS, HG_DK, HG_DV), lambda j: (layer, j, 0, 0, 0))
    return pl.pallas_call(
        _hgrn_sample_kernel,
        grid=(DEC_BATCH // HG_SB,),
        in_specs=[blk, blk, blk, blk,
                  pl.BlockSpec((None, 1, HG_DV), lambda j: (layer, 0, 0)),
                  s_blk, pl.BlockSpec(memory_space=pl.ANY)],
        out_specs=[pl.BlockSpec((rows, HG_WIDTH), lambda j: (j, 0)), s_blk],
        out_shape=[jax.ShapeDtypeStruct((N_SAMPLE, HG_WIDTH), _BF),
                   jax.ShapeDtypeStruct(all_states.shape, _F32)],
        input_output_aliases={6: 1},
        compiler_params=_cparams(1),
        name="hgrn_sample",
    )(q, g, iv, sg, hg_norm_w, state, all_states)


def _route(h2, h2_bf, wr_hi_ref, wr_lo_ref, br_ref, route_ref, route_t_ref, cnt_ref):
    h2_lo = (h2 - h2_bf.astype(_F32)).astype(_BF)
    logits = (_dot(h2_bf, wr_hi_ref[...]) + (_dot(h2_lo, wr_hi_ref[...]) + _dot(h2_bf, wr_lo_ref[...]))
              + br_ref[...])
    lane = lax.broadcasted_iota(_I32, logits.shape, 1)
    neg = jnp.float32(-jnp.inf)
    logits = jnp.where(lane < N_EXPERTS, logits, neg)
    m1 = jnp.max(logits, axis=-1, keepdims=True)
    i1 = jnp.min(jnp.where(logits == m1, lane, LANES), axis=-1, keepdims=True)
    rest_l = jnp.where(lane == i1, neg, logits)
    m2 = jnp.max(rest_l, axis=-1, keepdims=True)
    i2 = jnp.min(jnp.where(rest_l == m2, lane, LANES), axis=-1, keepdims=True)
    e2 = jnp.exp(m2 - m1)
    w1 = 1.0 / (1.0 + e2)
    w2 = e2 / (1.0 + e2)
    hot1 = lane == i1
    hot2 = lane == i2
    tot = jnp.where(jnp.logical_or(hot1, hot2), 1.0, 0.0)
    trow = lax.broadcasted_iota(_I32, (TM, TM), 0)
    tcol = lax.broadcasted_iota(_I32, (TM, TM), 1)
    before = _dot((trow > tcol).astype(_BF), tot.astype(_BF))
    cnt = jnp.sum(tot, axis=0, keepdims=True)
    cnt_pad = jnp.ceil(cnt * (1.0 / RUN_ALIGN)) * RUN_ALIGN
    erow = lax.broadcasted_iota(_I32, (LANES, LANES), 0)
    ecol = lax.broadcasted_iota(_I32, (LANES, LANES), 1)
    start = _dot(jnp.broadcast_to(cnt_pad, (SUBLANES, LANES)).astype(_BF),
                 (erow < ecol).astype(_BF))[0:1, :]
    slot = before + start
    pos1 = jnp.sum(jnp.where(hot1, slot, 0.0), axis=-1, keepdims=True)
    pos2 = jnp.sum(jnp.where(hot2, slot, 0.0), axis=-1, keepdims=True)
    route = jnp.where(lane == 0, pos1, jnp.where(lane == 1, pos2,
                      jnp.where(lane == 2, w1, jnp.where(lane == 3, w2, 0.0))))
    route_ref[...] = route
    route_t_ref[...] = route.T[0:SUBLANES, :]
    cnt_ref[...] = jnp.broadcast_to(cnt, (SUBLANES, LANES)).astype(_I32)


def _mix_kernel(with_router, split_x, u_ref, v_ref, sga_ref, sgb_ref, op_ref, os_ref, *rest):
    if split_x:
        xp_ref, xs_ref, *rest = rest
        x = jnp.where(_is_sample_tile(), xs_ref[...], xp_ref[...])
    else:
        x_ref, *rest = rest
        x = x_ref[...]
    (mp_g1, ms_g1, mp_sh, ms_sh, mp_sc, ms_sc,
     wmix_ref, bmix_ref, wa_ref, wb_ref, wo_ref, npost_ref, npre_ref, *rest) = rest
    if with_router:
        wr_hi_ref, wr_lo_ref, br_ref, x1_ref, h2_ref, route_ref, route_t_ref, cnt_ref, a_scr = rest
    else:
        x1_ref, h2_ref, a_scr = rest
    for c in range(TM // GM_CHUNK):
        rows = slice(c * GM_CHUNK, (c + 1) * GM_CHUNK)
        for gi in range(GM_GROUPS):
            cols = slice(gi * GM_GROUP_DIM, (gi + 1) * GM_GROUP_DIM)
            mixed = _dot(wmix_ref[gi], v_ref[rows, cols])
            bias = bmix_ref[gi]
            mixed = mixed + jnp.concatenate([bias] * (GM_GROUP_DIM // LANES), axis=1)
            a_scr[rows, cols] = (u_ref[rows, cols].astype(_F32) * mixed).astype(_BF)
    br_a = _dot(a_scr[...], wa_ref[...])
    o = jnp.where(_is_sample_tile(), os_ref[...], op_ref[...])
    br_b = _dot(o, wb_ref[...])
    merged = sga_ref[...].astype(_F32) * br_a + sgb_ref[...].astype(_F32) * br_b
    mix = _dot(merged.astype(_BF), wo_ref[...])
    g1 = _mod_rows(mp_g1, ms_g1)
    nm = _rms(mix, npost_ref[...])
    x1 = x + _per_seq(nm, lambda t: t * g1[:, None, :])
    x1_ref[...] = x1
    h2 = _modulated_norm(x1, npre_ref, mp_sh, ms_sh, mp_sc, ms_sc)
    h2_bf = h2.astype(_BF)
    h2_ref[...] = h2_bf
    if with_router:
        _route(h2, h2_bf, wr_hi_ref, wr_lo_ref, br_ref, route_ref, route_t_ref, cnt_ref)


def _mix_call(layer, moe_idx, u, v, sga, sgb, o_p, o_s, x, modp, mods, wmix, bmix,
              wa, wb, wo, npost, npre, wr_hi=None, wr_lo=None, br=None):
    with_router = moe_idx is not None
    split_x = isinstance(x, tuple)
    ty = lambda i: (i >= NT_P).astype(_I32)
    wspec = pl.BlockSpec((None, D_MODEL, D_MODEL), lambda i: (layer, 0, 0))
    in_specs = [_tile_spec(), _tile_spec(), _tile_spec(), _tile_spec()] + _split_specs(HG_WIDTH)
    in_specs += _split_specs() if split_x else [_tile_spec()]
    in_specs += _mod_specs(layer, 2) + _mod_specs(layer, 3) + _mod_specs(layer, 4)
    in_specs += [pl.BlockSpec((None, None, GM_GROUPS, GM_CHUNK, GM_CHUNK), lambda i: (layer, ty(i), 0, 0, 0)),
                 pl.BlockSpec((None, None, GM_GROUPS, GM_CHUNK, LANES), lambda i: (layer, ty(i), 0, 0, 0)),
                 wspec, wspec, wspec, _row_spec(layer), _row_spec(layer)]
    args = [u, v, sga, sgb, o_p, o_s] + (list(x) if split_x else [x])
    args += [modp, mods, modp, mods, modp, mods, wmix, bmix, wa, wb, wo, npost, npre]
    out_specs = [_tile_spec(), _tile_spec()]
    out_shape = [jax.ShapeDtypeStruct((N_TOK, D_MODEL), _F32),
                 jax.ShapeDtypeStruct((N_TOK, D_MODEL), _BF)]
    if with_router:
        rspec = pl.BlockSpec((None, D_MODEL, LANES), lambda i: (moe_idx, 0, 0))
        in_specs += [rspec, rspec, pl.BlockSpec((None, 1, LANES), lambda i: (moe_idx, 0, 0))]
        args += [wr_hi, wr_lo, br]
        out_specs += [_tile_spec(LANES),
                      pl.BlockSpec((None, SUBLANES, TM), lambda i: (i, 0, 0)),
                      pl.BlockSpec((None, SUBLANES, LANES), lambda i: (i, 0, 0))]
        out_shape += [jax.ShapeDtypeStruct((N_TOK, LANES), _F32),
                      jax.ShapeDtypeStruct((NT, SUBLANES, TM), _F32),
                      jax.ShapeDtypeStruct((NT, SUBLANES, LANES), _I32)]
    return pl.pallas_call(
        functools.partial(_mix_kernel, with_router, split_x),
        grid=(NT,),
        in_specs=in_specs,
        out_specs=out_specs,
        out_shape=out_shape,
        scratch_shapes=[pltpu.VMEM((TM, GM_WIDTH), _BF)],
        compiler_params=_cparams(1),
        name="mix",
    )(*args)


def _swiglu(h, wg_ref, wu_ref, wd_ref):
    acc = None
    for k in range(D_FF // FF_CHUNK):
        cs = slice(k * FF_CHUNK, (k + 1) * FF_CHUNK)
        gate = _dot(h, wg_ref[:, cs])
        up = _dot(h, wu_ref[:, cs])
        act = (_silu(gate) * up).astype(_BF)
        part = _dot(act, wd_ref[cs, :])
        acc = part if acc is None else acc + part
    return acc


def _layer_out(last, x1_ref, f, mp_g2, ms_g2, npost_ref, tail_refs):
    g2 = _mod_rows(mp_g2, ms_g2)
    nf = _rms(f, npost_ref[...])
    x2 = x1_ref[...] + _per_seq(nf, lambda t: t * g2[:, None, :])
    if last:
        yp_ref, ys_ref = tail_refs

        @pl.when(_is_sample_tile())
        def _():
            ys_ref[...] = x2

        @pl.when(jnp.logical_not(_is_sample_tile()))
        def _():
            yp_ref[...] = x2
    else:
        mp_sh, ms_sh, mp_sc, ms_sc, nw_ref, x_ref, h_ref = tail_refs
        x_ref[...] = x2
        h_ref[...] = _modulated_norm(x2, nw_ref, mp_sh, ms_sh, mp_sc, ms_sc).astype(_BF)


def _layer_out_specs(layer, last, modp, mods, npre_mix):
    if last:
        return [], [], _split_specs(), [jax.ShapeDtypeStruct((N_PROMPT, D_MODEL), _F32),
                                        jax.ShapeDtypeStruct((N_SAMPLE, D_MODEL), _F32)]
    in_specs = _mod_specs(layer + 1, 0) + _mod_specs(layer + 1, 1) + [_row_spec(layer + 1)]
    return (in_specs, [modp, mods, modp, mods, npre_mix], [_tile_spec(), _tile_spec()],
            [jax.ShapeDtypeStruct((N_TOK, D_MODEL), _F32), jax.ShapeDtypeStruct((N_TOK, D_MODEL), _BF)])


def _ffn_kernel(last, h_ref, x1_ref, mp_g2, ms_g2, wg_ref, wu_ref, wd_ref, npost_ref, *tail_refs):
    f = _swiglu(h_ref[...], wg_ref, wu_ref, wd_ref)
    _layer_out(last, x1_ref, f, mp_g2, ms_g2, npost_ref, tail_refs)


def _ffn_call(layer, j, h2, x1, modp, mods, wg, wu, wd, npost, npre_mix, rider=None):
    last = layer == DEPTH - 1
    extra_specs, extra_args, out_specs, out_shape = _layer_out_specs(layer, last, modp, mods, npre_mix)
    w_in = pl.BlockSpec((None, D_MODEL, D_FF), lambda i: (j, 0, 0), pipeline_mode=pl.Buffered(1))
    w_out = pl.BlockSpec((None, D_FF, D_MODEL), lambda i: (j, 0, 0), pipeline_mode=pl.Buffered(1))
    in_specs = [_tile_spec(), _tile_spec()] + _mod_specs(layer, 5) + [
        w_in, w_in, w_out, _row_spec(layer)] + extra_specs
    args = [h2, x1, modp, mods, wg, wu, wd, npost, *extra_args]
    body = _add_rider(rider, functools.partial(_ffn_kernel, last), len(args),
                      in_specs, args, out_specs, out_shape)
    return pl.pallas_call(
        body,
        grid=(NT,),
        in_specs=in_specs,
        out_specs=out_specs,
        out_shape=out_shape,
        compiler_params=_cparams(1),
        name="ffn",
    )(*args)


def _route_tables(cnt):
    cnt = cnt[:, 0, :N_EXPERTS]
    n = (cnt + (RUN_ALIGN - 1)) // RUN_ALIGN * RUN_ALIGN
    s = jnp.cumsum(n, axis=1) - n
    rows_e = jnp.sum(n, axis=0)
    region = (rows_e + (TG - 1)) // TG * TG
    region_end = jnp.cumsum(region)
    off = region_end - region
    p = off[None, :] + jnp.cumsum(n, axis=0) - n
    n_tiles = region_end[-1] // TG
    tile_row0 = jnp.minimum(jnp.arange(R_TILES, dtype=_I32), n_tiles - 1) * TG
    tile_expert = jnp.sum((tile_row0[:, None] >= region_end[None, :]).astype(_I32), axis=1)
    flat = lambda a: a.reshape(-1).astype(_I32)
    return dict(p=flat(p), s=flat(s), n=flat(n), tail_p=flat(off + rows_e), tail_n=flat(region - rows_e),
                tile_expert=flat(tile_expert), n_tiles=flat(n_tiles))


def _for_each_run_piece(n, src0, dst0, fn):
    for size in RUN_SIZES:
        done = n & (-2 * size)

        @pl.when((n & size) != 0)
        def _():
            fn(pl.multiple_of(src0 + done, RUN_ALIGN), pl.multiple_of(dst0 + done, RUN_ALIGN), size)


def _dispatch_kernel(p_tab, s_tab, n_tab, tail_p, tail_n, nt_ref, h_ref, rt_ref, xs_ref,
                     comp_scr, zero_scr, sem):
    t = pl.program_id(0)
    pos1 = rt_ref[0:1, :].astype(_I32)
    pos2 = rt_ref[1:2, :].astype(_I32)
    r = lax.broadcasted_iota(_I32, (CB, TM), 0)
    perm = jnp.where(jnp.logical_or(r == pos1, r == pos2), 1.0, 0.0).astype(_BF)
    comp_scr[...] = _dot(perm, h_ref[...]).astype(_BF)

    def runs(go):
        for e in range(N_EXPERTS):
            idx = t * N_EXPERTS + e

            def piece(src, dst, size):
                go(pltpu.make_async_copy(comp_scr.at[pl.ds(src, size)], xs_ref.at[pl.ds(dst, size)], sem))

            _for_each_run_piece(n_tab[idx], s_tab[idx], p_tab[idx], piece)

    runs(lambda cp: cp.start())
    runs(lambda cp: cp.wait())

    @pl.when(t == NT - 1)
    def _():
        zero_scr[...] = jnp.zeros_like(zero_scr)

        def tails(go):
            for e in range(N_EXPERTS):
                def piece(src, dst, size):
                    go(pltpu.make_async_copy(zero_scr.at[pl.ds(0, size)], xs_ref.at[pl.ds(dst, size)], sem))

                _for_each_run_piece(tail_n[e], 0, tail_p[e], piece)

        tails(lambda cp: cp.start())
        tails(lambda cp: cp.wait())

        def spare_tile(go):
            def body(r, carry):
                dst = pl.multiple_of(r * TG, TG)
                go(pltpu.make_async_copy(zero_scr, xs_ref.at[pl.ds(dst, TG)], sem))
                return carry
            lax.fori_loop(nt_ref[0], R_TILES, body, 0)

        spare_tile(lambda cp: cp.start())
        spare_tile(lambda cp: cp.wait())


def _dispatch_call(tabs, h2, route_t):
    grid_spec = pltpu.PrefetchScalarGridSpec(
        num_scalar_prefetch=6,
        grid=(NT,),
        in_specs=[_tile_spec(), pl.BlockSpec((None, SUBLANES, TM), lambda i, *_: (i, 0, 0))],
        out_specs=pl.BlockSpec(memory_space=pl.ANY),
        scratch_shapes=[pltpu.VMEM((CB, D_MODEL), _BF), pltpu.VMEM((TG, D_MODEL), _BF),
                        pltpu.SemaphoreType.DMA(())])
    return pl.pallas_call(
        _dispatch_kernel,
        grid_spec=grid_spec,
        out_shape=jax.ShapeDtypeStruct((N_ROWS, D_MODEL), _BF),
        compiler_params=_cparams(1),
        name="dispatch",
    )(tabs["p"], tabs["s"], tabs["n"], tabs["tail_p"], tabs["tail_n"], tabs["n_tiles"], h2, route_t)


def _expert_kernel(te_ref, nt_ref, x_ref, wg_ref, wu_ref, wd_ref, y_ref):
    r = pl.program_id(0)

    @pl.when(r < nt_ref[0])
    def _():
        y_ref[...] = _swiglu(x_ref[...], wg_ref, wu_ref, wd_ref).astype(_BF)

    @pl.when(r >= nt_ref[0])
    def _():
        y_ref[...] = jnp.zeros_like(y_ref)


def _expert_call(tabs, xs, wg, wu, wd):
    wg, wu = wg.reshape(N_EXPERTS, D_MODEL, D_FF), wu.reshape(N_EXPERTS, D_MODEL, D_FF)
    wd = wd.reshape(N_EXPERTS, D_FF, D_MODEL)
    w_in = pl.BlockSpec((None, D_MODEL, D_FF), lambda r, te, nt: (te[r], 0, 0))
    w_out = pl.BlockSpec((None, D_FF, D_MODEL), lambda r, te, nt: (te[r], 0, 0))
    grid_spec = pltpu.PrefetchScalarGridSpec(
        num_scalar_prefetch=2,
        grid=(R_TILES,),
        in_specs=[pl.BlockSpec((TG, D_MODEL), lambda r, te, nt: (jnp.minimum(r, nt[0] - 1), 0)),
                  w_in, w_in, w_out],
        out_specs=pl.BlockSpec((TG, D_MODEL), lambda r, te, nt: (r, 0)))
    return pl.pallas_call(
        _expert_kernel,
        grid_spec=grid_spec,
        out_shape=jax.ShapeDtypeStruct((N_ROWS, D_MODEL), _BF),
        compiler_params=_cparams(1),
        name="experts",
    )(tabs["tile_expert"], tabs["n_tiles"], xs, wg, wu, wd)


def _combine_kernel(last, p_tab, s_tab, n_tab, route_ref, x1_ref, mp_g2, ms_g2, npost_ref, ys_ref, *rest):
    *tail_refs, yc_scr, sem = rest
    t = pl.program_id(0)
    yc_scr[...] = jnp.zeros_like(yc_scr)

    def runs(go):
        for e in range(N_EXPERTS):
            idx = t * N_EXPERTS + e

            def piece(src, dst, size):
                go(pltpu.make_async_copy(ys_ref.at[pl.ds(dst, size)], yc_scr.at[pl.ds(src, size)], sem))

            _for_each_run_piece(n_tab[idx], s_tab[idx], p_tab[idx], piece)

    runs(lambda cp: cp.start())
    runs(lambda cp: cp.wait())
    route = route_ref[...]
    pos1 = route[:, 0:1].astype(_I32)
    pos2 = route[:, 1:2].astype(_I32)
    w1 = route[:, 2:3]
    w2 = route[:, 3:4]
    slot = lax.broadcasted_iota(_I32, (TM, CB), 1)
    yc = yc_scr[...]
    pick1 = jnp.where(slot == pos1, 1.0, 0.0).astype(_BF)
    pick2 = jnp.where(slot == pos2, 1.0, 0.0).astype(_BF)
    f = w1 * _dot(pick1, yc) + w2 * _dot(pick2, yc)
    _layer_out(last, x1_ref, f, mp_g2, ms_g2, npost_ref, tail_refs)


def _combine_call(layer, tabs, route, x1, modp, mods, npost, ys, npre_mix):
    last = layer == DEPTH - 1
    extra_specs, extra_args, out_specs, out_shape = _layer_out_specs(layer, last, modp, mods, npre_mix)
    grid_spec = pltpu.PrefetchScalarGridSpec(
        num_scalar_prefetch=3,
        grid=(NT,),
        in_specs=[_tile_spec(LANES), _tile_spec()] + _mod_specs(layer, 5) + [
            _row_spec(layer), pl.BlockSpec(memory_space=pl.ANY)] + extra_specs,
        out_specs=out_specs,
        scratch_shapes=[pltpu.VMEM((CB, D_MODEL), _BF), pltpu.SemaphoreType.DMA(())])
    return pl.pallas_call(
        functools.partial(_combine_kernel, last),
        grid_spec=grid_spec,
        out_shape=out_shape,
        compiler_params=_cparams(1),
        name="combine",
    )(tabs["p"], tabs["s"], tabs["n"], route, x1, modp, mods, npost, ys, *extra_args)


def _spatial_tables(gm_ws, gm_bs):
    mask_p = jnp.tril(jnp.ones((GM_CHUNK, GM_CHUNK), bool))
    w_p = jnp.where(mask_p, gm_ws, 0.0)
    mask_s = jnp.tril(jnp.ones((DEC_SEQ, DEC_SEQ), bool))
    w_small = jnp.where(mask_s, gm_ws[:, :, :DEC_SEQ, :DEC_SEQ], 0.0)
    eye = jnp.eye(GM_CHUNK // DEC_SEQ, dtype=gm_ws.dtype)
    w_s = jnp.einsum("ab,lgts->lgatbs", eye, w_small).reshape(gm_ws.shape)
    b_p = gm_bs
    b_s = jnp.tile(gm_bs[:, :, :DEC_SEQ], (1, 1, GM_CHUNK // DEC_SEQ))
    wmix = jnp.stack([w_p, w_s], axis=1).astype(_BF)
    bmix = jnp.stack([b_p, b_s], axis=1)[..., None]
    bmix = jnp.broadcast_to(bmix, bmix.shape[:-1] + (LANES,)).astype(_F32)
    return wmix, bmix


def kernel(x_prompt, x_sample, c_prompt, c_sample, state_hgrn, w_in, gm_ln_w, gm_ln_b, gm_ws, gm_bs,
           hg_lb_raw, hg_norm_w, w_branch_a, w_branch_b, w_out, w_ada, b_ada,
           norm_pre_mix, norm_post_mix, norm_pre_ffn, norm_post_ffn,
           w_ffn_gate, w_ffn_up, w_ffn_down, w_router, b_router, w_exp_gate, w_exp_up, w_exp_down):
    x = (x_prompt.reshape(N_PROMPT, D_MODEL), x_sample.reshape(N_SAMPLE, D_MODEL))
    c_all = jnp.concatenate([c_prompt, c_sample], axis=0)
    mod = _ada_call(c_all, w_ada, b_ada)
    modp = mod[:, :BATCH].reshape(DEPTH, BATCH, 1, 6 * D_MODEL)
    mods = mod[:, BATCH:]
    loglb = _lb_call(hg_lb_raw)
    aux = jnp.zeros((DEPTH, SUBLANES, D_MODEL), _F32)
    aux = aux.at[:, 0].set(gm_ln_w).at[:, 1].set(gm_ln_b).at[:, 2].set(loglb)
    wmix, bmix = _spatial_tables(gm_ws, gm_bs)
    row3 = lambda a: a.reshape(a.shape[0], 1, a.shape[1])
    w_in_bf = w_in.astype(_BF)
    wa_bf, wb_bf, wo_bf = w_branch_a.astype(_BF), w_branch_b.astype(_BF), w_out.astype(_BF)
    wfg, wfu, wfd = w_ffn_gate.astype(_BF), w_ffn_up.astype(_BF), w_ffn_down.astype(_BF)
    wr_pad = jnp.pad(w_router, ((0, 0), (0, 0), (0, LANES - N_EXPERTS)))
    wr_hi = wr_pad.astype(_BF)
    wr_lo = (wr_pad - wr_hi.astype(_F32)).astype(_BF)
    br_pad = jnp.pad(b_router, ((0, 0), (0, LANES - N_EXPERTS))).reshape(-1, 1, LANES)
    npre_mix, npost_mix = row3(norm_pre_mix), row3(norm_post_mix)
    npre_ffn, npost_ffn = row3(norm_pre_ffn), row3(norm_post_ffn)
    hg_nw = row3(hg_norm_w)

    hp, vs = [], []
    sample_states = jnp.zeros(state_hgrn.shape, _F32)
    h = _prenorm_call(0, x[0], x[1], modp, mods, npre_mix)
    for l in range(DEPTH):
        j = l // 2
        dense = l % 2 == 0
        ride = dense and l + 1 < DEPTH
        rider = (lambda w: (w, (l + 1) // 2)) if ride else (lambda w: None)
        act0 = list(_inproj_call(l, 0, ("gelu", "gelu_ln", "silu", "logf"), h, w_in_bf, aux, TM,
                                 rider(w_exp_gate)))
        act4 = list(_inproj_call(l, 4, ("id", "silu", "sigmoid", "sigmoid"), h, w_in_bf, aux, INPROJ_ROWS,
                                 rider(w_exp_up)))
        if ride:
            expert_w = [act0.pop(), act4.pop()]
        u, v, v32, q, g = act0
        iv, sg, sga, sgb = act4
        o_p, s_p = _hgrn_prompt_call(l, q, g, iv, sg, hg_nw)
        o_s, sample_states = _hgrn_sample_call(l, q, g, iv, sg, hg_nw, state_hgrn, sample_states)
        if dense:
            x1, h2 = _mix_call(l, None, u, v, sga, sgb, o_p, o_s, x, modp, mods, wmix, bmix,
                               wa_bf, wb_bf, wo_bf, npost_mix, npre_ffn)
            out = list(_ffn_call(l, j, h2, x1, modp, mods, wfg, wfu, wfd, npost_ffn, npre_mix,
                                 rider(w_exp_down)))
            if ride:
                expert_w.append(out.pop())
        else:
            x1, h2, route, route_t, cnt = _mix_call(l, j, u, v, sga, sgb, o_p, o_s, x, modp, mods, wmix, bmix,
                                                    wa_bf, wb_bf, wo_bf, npost_mix, npre_ffn,
                                                    wr_hi, wr_lo, br_pad)
            tabs = _route_tables(cnt)
            xs = _dispatch_call(tabs, h2, route_t)
            ys = _expert_call(tabs, xs, *expert_w)
            out = _combine_call(l, tabs, route, x1, modp, mods, npost_ffn, ys, npre_mix)
        if l < DEPTH - 1:
            x, h = out
        hp.append(s_p)
        vs.append(v32.reshape(DEC_BATCH, DEC_SEQ, GM_WIDTH))
    y_prompt = out[0].reshape(BATCH, SEQ, D_MODEL)
    y_sample = out[1].reshape(DEC_BATCH, DEC_SEQ, D_MODEL)
    return (y_prompt, y_sample, jnp.stack(hp, axis=0), sample_states, jnp.stack(vs, axis=0))
```

```python
import functools
import math

import jax
import jax.numpy as jnp
from jax import lax
from jax.experimental import pallas as pl
from jax.experimental.pallas import tpu as pltpu

D_MODEL = 1024
BATCH = 8
SEQ = 2048
DEPTH = 4
DEC_BATCH = 128
DEC_SEQ = 8
GM_WIDTH = 1024
GM_GROUPS = 4
GM_GROUP_DIM = GM_WIDTH // GM_GROUPS
GM_CHUNK = 128
HG_HEADS = 8
HG_DK = 128
HG_DV = 128
HG_WIDTH = HG_HEADS * HG_DK
D_FF = 2816
N_EXPERTS = 8
TOP_K = 2
IN_COLS = 2 * GM_WIDTH + 4 * HG_WIDTH + 2 * D_MODEL
EPS = 1e-6

LANES = 128
SUBLANES = 8
BF16_ROWS = 16
TM = 512
N_PROMPT = BATCH * SEQ
N_SAMPLE = DEC_BATCH * DEC_SEQ
N_TOK = N_PROMPT + N_SAMPLE
NT_P = N_PROMPT // TM
NT_S = N_SAMPLE // TM
NT = NT_P + NT_S
TILES_PER_SEQ = SEQ // TM
SEQ_PER_TILE = TM // DEC_SEQ
HG_C = 64
HG_BLK = 256
HG_SB = 8
FFN_ROWS = 256
INPROJ_ROWS = 256
RIDER_BLOCKS = 32
EXP_RANGE = 70.0
VMEM_LIMIT = 56 * 1024 * 1024

TG = 512
RUN_ALIGN = BF16_ROWS
RUN_SIZES = tuple(TM >> k for k in range((TM // RUN_ALIGN).bit_length()))
CB = TOP_K * TM + N_EXPERTS * RUN_ALIGN
_ROWS_BOUND = TOP_K * N_TOK + NT * N_EXPERTS * (RUN_ALIGN - 1) + N_EXPERTS * (TG - RUN_ALIGN)
R_TILES = -(-_ROWS_BOUND // TG)
N_ROWS = R_TILES * TG

_BF = jnp.bfloat16
_F32 = jnp.float32
_I32 = jnp.int32


def _cparams(n_axes):
    return pltpu.CompilerParams(dimension_semantics=("arbitrary",) * n_axes,
                                vmem_limit_bytes=VMEM_LIMIT)


def _dot(a, b):
    return jnp.dot(a, b, preferred_element_type=_F32)


def _sigmoid(x):
    return 0.5 + 0.5 * jnp.tanh(0.5 * x)


def _silu(x):
    hx = 0.5 * x
    return hx + hx * jnp.tanh(hx)


def _gelu(x):
    return 0.5 * x * (1.0 + lax.erf(x * (1.0 / math.sqrt(2.0))))


def _rms(x, w):
    return x * lax.rsqrt(jnp.mean(x * x, axis=-1, keepdims=True) + EPS) * w


def _is_sample_tile():
    return pl.program_id(0) >= NT_P


def _mod_rows(mp_ref, ms_ref):
    mp = jnp.broadcast_to(mp_ref[...], ms_ref.shape)
    return jnp.where(_is_sample_tile(), ms_ref[...], mp)


def _per_seq(x, fn_rows):
    x3 = x.reshape(SEQ_PER_TILE, DEC_SEQ, x.shape[-1])
    return fn_rows(x3).reshape(TM, x.shape[-1])


def _mod_specs(layer, col_block):
    mp = pl.BlockSpec((None, None, 1, D_MODEL),
                      lambda i, *_: (layer, jnp.minimum(i // TILES_PER_SEQ, BATCH - 1), 0, col_block))
    ms = pl.BlockSpec((None, SEQ_PER_TILE, D_MODEL),
                      lambda i, *_: (layer, jnp.maximum(i - NT_P, 0), col_block))
    return [mp, ms]


def _tile_spec(width=D_MODEL):
    return pl.BlockSpec((TM, width), lambda i, *_: (i, 0))


def _row_spec(layer, width=D_MODEL):
    return pl.BlockSpec((None, 1, width), lambda i, *_: (layer, 0, 0))


def _ada_kernel(c_ref, w_ref, b_ref, o_ref):
    s = _silu(c_ref[...]).astype(_BF)
    o_ref[...] = _dot(s, w_ref[...].astype(_BF)) + b_ref[...]


def _ada_call(c_all, w_ada, b_ada):
    nb = 4
    wcol = 6 * D_MODEL // nb
    rows = c_all.shape[0]
    return pl.pallas_call(
        _ada_kernel,
        grid=(DEPTH, nb),
        in_specs=[pl.BlockSpec((rows, D_MODEL), lambda l, j: (0, 0)),
                  pl.BlockSpec((None, D_MODEL, wcol), lambda l, j: (l, 0, j)),
                  pl.BlockSpec((None, 1, wcol), lambda l, j: (l, 0, j))],
        out_specs=pl.BlockSpec((None, rows, wcol), lambda l, j: (l, 0, j)),
        out_shape=jax.ShapeDtypeStruct((DEPTH, rows, 6 * D_MODEL), _F32),
        compiler_params=_cparams(2),
        name="ada",
    )(c_all, w_ada, b_ada.reshape(DEPTH, 1, 6 * D_MODEL))


def _lb_kernel(raw_ref, loglb_ref):
    raw = raw_ref[...]
    m = jnp.max(raw, axis=0, keepdims=True)
    e = jnp.exp(raw - m)
    p = e / jnp.sum(e, axis=0, keepdims=True)
    acc = jnp.zeros((1, HG_WIDTH), _F32)
    for l in range(DEPTH):
        if l > 0:
            acc = acc + p[l:l + 1, :]
        loglb_ref[l:l + 1, :] = jnp.log(acc)


def _lb_call(raw):
    return pl.pallas_call(
        _lb_kernel,
        out_shape=jax.ShapeDtypeStruct((DEPTH, HG_WIDTH), _F32),
        name="lower_bounds",
    )(raw)


def _modulated_norm(x, nw_ref, mp_sh, ms_sh, mp_sc, ms_sc):
    xn = _rms(x, nw_ref[...])
    sh = _mod_rows(mp_sh, ms_sh)
    sc = _mod_rows(mp_sc, ms_sc)
    return _per_seq(xn, lambda x3: x3 * (1.0 + sc)[:, None, :] + sh[:, None, :])


def _prenorm_kernel(xp_ref, xs_ref, mp_sh, ms_sh, mp_sc, ms_sc, nw_ref, h_ref):
    x = jnp.where(_is_sample_tile(), xs_ref[...], xp_ref[...])
    h_ref[...] = _modulated_norm(x, nw_ref, mp_sh, ms_sh, mp_sc, ms_sc).astype(_BF)


def _split_specs(width=D_MODEL):
    return [pl.BlockSpec((TM, width), lambda i, *_: (jnp.minimum(i, NT_P - 1), 0)),
            pl.BlockSpec((TM, width), lambda i, *_: (jnp.maximum(i - NT_P, 0), 0))]


def _prenorm_call(layer, x_p, x_s, modp, mods, norm_w):
    return pl.pallas_call(
        _prenorm_kernel,
        grid=(NT,),
        in_specs=_split_specs() + _mod_specs(layer, 0) + _mod_specs(layer, 1) + [_row_spec(layer)],
        out_specs=_tile_spec(),
        out_shape=jax.ShapeDtypeStruct((N_TOK, D_MODEL), _BF),
        compiler_params=_cparams(1),
        name="prenorm",
    )(x_p, x_s, modp, mods, modp, mods, norm_w)


def _with_cast_rider(body, n_in):
    def wrapped(*refs):
        rider_in, rider_out = refs[n_in], refs[-1]
        rider_out[...] = rider_in[...].astype(_BF)
        body(*refs[:n_in], *refs[n_in + 1:-1])
    return wrapped


def _add_rider(rider, body, n_in, in_specs, args, out_specs, out_shape):
    if rider is None:
        return body
    w, j = rider
    w2 = w.reshape(-1, w.shape[-1])
    rows_per_layer = w2.shape[0] // w.shape[0]
    rows = rows_per_layer // RIDER_BLOCKS
    blk = lambda i: jnp.minimum(i, RIDER_BLOCKS - 1)
    in_specs.append(pl.BlockSpec((rows, w2.shape[1]), lambda i, *_: (j * RIDER_BLOCKS + blk(i), 0)))
    args.append(w2)
    out_specs.append(pl.BlockSpec((rows, w2.shape[1]), lambda i, *_: (blk(i), 0)))
    out_shape.append(jax.ShapeDtypeStruct((rows_per_layer, w2.shape[1]), _BF))
    return _with_cast_rider(body, n_in)


def _inproj_kernel(kinds, row_block, h_ref, w_ref, aux_ref, *out_refs):
    for rb in range(TM // row_block):
        rs = slice(rb * row_block, (rb + 1) * row_block)
        h = h_ref[rs, :]
        outs = list(out_refs)
        for j, kind in enumerate(kinds):
            z = _dot(h, w_ref[:, j * D_MODEL:(j + 1) * D_MODEL])
            if kind == "gelu":
                outs.pop(0)[rs, :] = _gelu(z).astype(_BF)
            elif kind == "gelu_ln":
                a = _gelu(z)
                mu = jnp.mean(a, axis=-1, keepdims=True)
                ac = a - mu
                vn = ac * lax.rsqrt(jnp.mean(ac * ac, axis=-1, keepdims=True) + EPS)
                vn = vn * aux_ref[0:1, :] + aux_ref[1:2, :]
                outs.pop(0)[rs, :] = vn.astype(_BF)
                v32_ref = outs.pop(0)

                @pl.when(_is_sample_tile())
                def _():
                    v32_ref[rs, :] = vn
            elif kind == "silu":
                outs.pop(0)[rs, :] = _silu(z).astype(_BF)
            elif kind == "sigmoid":
                outs.pop(0)[rs, :] = _sigmoid(z).astype(_BF)
            elif kind == "id":
                outs.pop(0)[rs, :] = z.astype(_BF)
            elif kind == "logf":
                loglb = aux_ref[2:3, :]
                tail = jnp.log(1.0 + jnp.exp(-jnp.abs(z)))
                a = jnp.minimum(z, 0.0) - tail
                c = loglb - jnp.maximum(z, 0.0) - tail
                outs.pop(0)[rs, :] = jnp.maximum(a, c) + jnp.log(1.0 + jnp.exp(-jnp.abs(z - loglb)))
            else:
                raise ValueError(kind)


def _inproj_call(layer, sec0, kinds, h, w_in_bf, aux, row_block, rider=None):
    nsec = len(kinds)
    out_specs, out_shape = [], []
    for kind in kinds:
        dt = _F32 if kind == "logf" else _BF
        out_specs.append(_tile_spec())
        out_shape.append(jax.ShapeDtypeStruct((N_TOK, D_MODEL), dt))
        if kind == "gelu_ln":
            out_specs.append(pl.BlockSpec((TM, D_MODEL), lambda i: (jnp.maximum(i - NT_P, 0), 0)))
            out_shape.append(jax.ShapeDtypeStruct((N_SAMPLE, D_MODEL), _F32))
    in_specs = [_tile_spec(),
                pl.BlockSpec((None, D_MODEL, nsec * D_MODEL), lambda i: (layer, 0, sec0 // nsec)),
                pl.BlockSpec((None, SUBLANES, D_MODEL), lambda i: (layer, 0, 0))]
    args = [h, w_in_bf, aux]
    body = _add_rider(rider, functools.partial(_inproj_kernel, kinds, row_block), len(args),
                      in_specs, args, out_specs, out_shape)
    return pl.pallas_call(
        body,
        grid=(NT,),
        in_specs=in_specs,
        out_specs=out_specs,
        out_shape=out_shape,
        compiler_params=_cparams(1),
        name=f"inproj_{sec0}",
    )(*args)


def _col_bcast(row):
    n = row.shape[-1]
    return jnp.broadcast_to(row, (n, n)).T


def _head_out(o, nw, sg):
    return (_rms(o, nw) * sg.astype(_F32)).astype(_BF)


def _hgrn_exact(ns, rows, q_ref, g_ref, i_ref, sg_ref, nw_ref, o_ref, s_get, s_put):
    n = ns * DEC_SEQ
    shape3 = (ns, DEC_SEQ, HG_WIDTH)
    t_row = jnp.bitwise_and(lax.broadcasted_iota(_I32, (n, HG_WIDTH), 0), DEC_SEQ - 1)
    t3 = lax.broadcasted_iota(_I32, (ns, DEC_SEQ, 1), 1)
    g = g_ref[rows, :]
    b = g
    shift = 1
    while shift < DEC_SEQ:
        b = b + jnp.where(t_row >= shift, pltpu.roll(b, shift, axis=0), 0.0)
        shift *= 2
    kk = 1.0 - jnp.exp(g)
    q = q_ref[rows, :].astype(_F32)
    b3 = b.reshape(shape3)
    bend3 = b3[:, DEC_SEQ - 1:DEC_SEQ, :]
    qe = (q * jnp.exp(b)).astype(_BF)
    kdec = (kk.reshape(shape3) * jnp.exp(bend3 - b3)).reshape(n, HG_WIDTH).astype(_BF)
    e_end = jnp.exp(bend3)
    iv_all = i_ref[rows, :]
    sg_all = sg_ref[rows, :]
    q3, k3, i3 = q.reshape(shape3), kk.reshape(shape3), iv_all.astype(_F32).reshape(shape3)
    for h in range(HG_HEADS):
        hs = slice(h * HG_DK, (h + 1) * HG_DK)
        qh, bh, kh, ih = q3[:, :, hs], b3[:, :, hs], k3[:, :, hs], i3[:, :, hs]
        intra = jnp.zeros((ns, DEC_SEQ, HG_DV), _F32)
        for j in range(DEC_SEQ):
            decay = jnp.exp(jnp.minimum(bh - bh[:, j:j + 1, :], 0.0))
            w = jnp.sum(qh * decay * kh[:, j:j + 1, :], axis=-1, keepdims=True)
            intra = intra + jnp.where(t3 >= j, w, 0.0) * ih[:, j:j + 1, :]
        outs = []
        for si in range(ns):
            rs = slice(si * DEC_SEQ, (si + 1) * DEC_SEQ)
            s_old = s_get(si, h)
            outs.append(intra[si] + _dot(qe[rs, hs], s_old.astype(_BF)))
            upd = lax.dot_general(kdec[rs, hs], iv_all[rs, hs], (((0,), (0,)), ((), ())),
                                  preferred_element_type=_F32)
            s_put(si, h, _col_bcast(e_end[si, :, hs]) * s_old + upd)
        o = outs[0] if ns == 1 else jnp.concatenate(outs, axis=0)
        o_ref[rows, hs] = _head_out(o, nw_ref[...], sg_all[:, hs])


def _rows_of_chunks(x, row_in_chunk):
    parts = []
    for c in range(HG_BLK // HG_C):
        r = c * HG_C + row_in_chunk
        parts.append(jnp.broadcast_to(x[r:r + 1, :], (HG_C, x.shape[-1])))
    return jnp.concatenate(parts, axis=0)


def _hgrn_prompt_kernel(q_ref, g_ref, i_ref, sg_ref, nw_ref, o_ref, s_out_ref, s_scr):
    c = pl.program_id(1)

    @pl.when(c == 0)
    def _():
        s_scr[...] = jnp.zeros_like(s_scr)

    g = g_ref[...]
    row = lax.broadcasted_iota(_I32, (HG_BLK, HG_BLK), 0)
    col = lax.broadcasted_iota(_I32, (HG_BLK, HG_BLK), 1)
    tril = jnp.logical_and(row >= col, row // HG_C == col // HG_C).astype(_BF)
    g_hi = g.astype(_BF)
    r1 = g - g_hi.astype(_F32)
    g_mid = r1.astype(_BF)
    g_lo = (r1 - g_mid.astype(_F32)).astype(_BF)
    b = _dot(tril, g_hi) + _dot(tril, g_mid) + _dot(tril, g_lo)
    ref = _rows_of_chunks(b, HG_C // 2 - 1)
    in_range = jnp.max(jnp.abs(b - ref)) <= EXP_RANGE

    @pl.when(in_range)
    def _():
        kk = 1.0 - jnp.exp(g)
        q = q_ref[...].astype(_F32)
        bend = _rows_of_chunks(b, HG_C - 1)
        qt = (q * jnp.exp(b - ref)).astype(_BF)
        kt = (kk * jnp.exp(ref - b)).astype(_BF)
        qe = (q * jnp.exp(b)).astype(_BF)
        kdec = (kk * jnp.exp(bend - b)).astype(_BF)
        e_end = jnp.exp(bend)
        crow = lax.broadcasted_iota(_I32, (HG_C, HG_C), 0)
        ccol = lax.broadcasted_iota(_I32, (HG_C, HG_C), 1)
        causal = crow >= ccol
        for h in range(HG_HEADS):
            hs = slice(h * HG_DK, (h + 1) * HG_DK)
            s_cur = s_scr[h]
            for ci in range(HG_BLK // HG_C):
                rs = slice(ci * HG_C, (ci + 1) * HG_C)
                iv = i_ref[rs, hs]
                scores = lax.dot_general(qt[rs, hs], kt[rs, hs], (((1,), (1,)), ((), ())),
                                         preferred_element_type=_F32)
                scores = jnp.where(causal, scores, 0.0).astype(_BF)
                o = _dot(scores, iv) + _dot(qe[rs, hs], s_cur.astype(_BF))
                upd = lax.dot_general(kdec[rs, hs], iv, (((0,), (0,)), ((), ())),
                                      preferred_element_type=_F32)
                s_cur = _col_bcast(e_end[ci * HG_C:ci * HG_C + 1, hs]) * s_cur + upd
                o_ref[rs, hs] = _head_out(o, nw_ref[...], sg_ref[rs, hs])
            s_scr[h] = s_cur

    @pl.when(jnp.logical_not(in_range))
    def _():
        def s_put(si, h, val):
            s_scr[h] = val

        def sub_chunk(k, carry):
            rows = pl.ds(pl.multiple_of(k * DEC_SEQ, DEC_SEQ), DEC_SEQ)
            _hgrn_exact(1, rows, q_ref, g_ref, i_ref, sg_ref, nw_ref, o_ref, lambda si, h: s_scr[h], s_put)
            return carry

        lax.fori_loop(0, HG_BLK // DEC_SEQ, sub_chunk, 0)

    @pl.when(c == pl.num_programs(1) - 1)
    def _():
        s_out_ref[...] = s_scr[...]


def _hgrn_prompt_call(layer, q, g, iv, sg, hg_norm_w):
    nc = SEQ // HG_BLK
    blk = pl.BlockSpec((HG_BLK, HG_WIDTH), lambda b, c: (b * nc + c, 0))
    return pl.pallas_call(
        _hgrn_prompt_kernel,
        grid=(BATCH, nc),
        in_specs=[blk, blk, blk, blk,
                  pl.BlockSpec((None, 1, HG_DV), lambda b, c: (layer, 0, 0))],
        out_specs=[blk,
                   pl.BlockSpec((None, HG_HEADS, HG_DK, HG_DV), lambda b, c: (b, 0, 0, 0))],
        out_shape=[jax.ShapeDtypeStruct((N_PROMPT, HG_WIDTH), _BF),
                   jax.ShapeDtypeStruct((BATCH, HG_HEADS, HG_DK, HG_DV), _F32)],
        scratch_shapes=[pltpu.VMEM((HG_HEADS, HG_DK, HG_DV), _F32)],
        compiler_params=_cparams(2),
        name="hgrn_prompt",
    )(q, g, iv, sg, hg_norm_w)


def _hgrn_sample_kernel(q_ref, g_ref, i_ref, sg_ref, nw_ref, s_in_ref, all_states_ref, o_ref, s_out_ref):
    del all_states_ref

    def s_put(si, h, val):
        s_out_ref[si, h] = val

    _hgrn_exact(HG_SB, slice(None), q_ref, g_ref, i_ref, sg_ref, nw_ref, o_ref, lambda si, h: s_in_ref[si, h], s_put)


def _hgrn_sample_call(layer, q, g, iv, sg, hg_norm_w, state, all_states):
    rows = HG_SB * DEC_SEQ
    off = N_PROMPT // rows
    blk = pl.BlockSpec((rows, HG_WIDTH), lambda j: (off + j, 0))
    s_blk = pl.BlockSpec((None, HG_SB, HG_HEADS, HG_DK, HG_DV), lambda j: (layer, j, 0, 0, 0))
    return pl.pallas_call(
        _hgrn_sample_kernel,
        grid=(DEC_BATCH // HG_SB,),
        in_specs=[blk, blk, blk, blk,
                  pl.BlockSpec((None, 1, HG_DV), lambda j: (layer, 0, 0)),
                  s_blk, pl.BlockSpec(memory_space=pl.ANY)],
        out_specs=[pl.BlockSpec((rows, HG_WIDTH), lambda j: (j, 0)), s_blk],
        out_shape=[jax.ShapeDtypeStruct((N_SAMPLE, HG_WIDTH), _BF),
                   jax.ShapeDtypeStruct(all_states.shape, _F32)],
        input_output_aliases={6: 1},
        compiler_params=_cparams(1),
        name="hgrn_sample",
    )(q, g, iv, sg, hg_norm_w, state, all_states)


def _route(h2, h2_bf, wr_hi_ref, wr_lo_ref, br_ref, route_ref, route_t_ref, cnt_ref):
    h2_lo = (h2 - h2_bf.astype(_F32)).astype(_BF)
    logits = (_dot(h2_bf, wr_hi_ref[...]) + (_dot(h2_lo, wr_hi_ref[...]) + _dot(h2_bf, wr_lo_ref[...]))
              + br_ref[...])
    lane = lax.broadcasted_iota(_I32, logits.shape, 1)
    neg = jnp.float32(-jnp.inf)
    logits = jnp.where(lane < N_EXPERTS, logits, neg)
    m1 = jnp.max(logits, axis=-1, keepdims=True)
    i1 = jnp.min(jnp.where(logits == m1, lane, LANES), axis=-1, keepdims=True)
    rest_l = jnp.where(lane == i1, neg, logits)
    m2 = jnp.max(rest_l, axis=-1, keepdims=True)
    i2 = jnp.min(jnp.where(rest_l == m2, lane, LANES), axis=-1, keepdims=True)
    e2 = jnp.exp(m2 - m1)
    w1 = 1.0 / (1.0 + e2)
    w2 = e2 / (1.0 + e2)
    hot1 = lane == i1
    hot2 = lane == i2
    tot = jnp.where(jnp.logical_or(hot1, hot2), 1.0, 0.0)
    trow = lax.broadcasted_iota(_I32, (TM, TM), 0)
    tcol = lax.broadcasted_iota(_I32, (TM, TM), 1)
    before = _dot((trow > tcol).astype(_BF), tot.astype(_BF))
    cnt = jnp.sum(tot, axis=0, keepdims=True)
    cnt_pad = jnp.ceil(cnt * (1.0 / RUN_ALIGN)) * RUN_ALIGN
    erow = lax.broadcasted_iota(_I32, (LANES, LANES), 0)
    ecol = lax.broadcasted_iota(_I32, (LANES, LANES), 1)
    start = _dot(jnp.broadcast_to(cnt_pad, (SUBLANES, LANES)).astype(_BF),
                 (erow < ecol).astype(_BF))[0:1, :]
    slot = before + start
    pos1 = jnp.sum(jnp.where(hot1, slot, 0.0), axis=-1, keepdims=True)
    pos2 = jnp.sum(jnp.where(hot2, slot, 0.0), axis=-1, keepdims=True)
    route = jnp.where(lane == 0, pos1, jnp.where(lane == 1, pos2,
                      jnp.where(lane == 2, w1, jnp.where(lane == 3, w2, 0.0))))
    route_ref[...] = route
    route_t_ref[...] = route.T[0:SUBLANES, :]
    cnt_ref[...] = jnp.broadcast_to(cnt, (SUBLANES, LANES)).astype(_I32)


def _mix_kernel(with_router, split_x, u_ref, v_ref, sga_ref, sgb_ref, op_ref, os_ref, *rest):
    if split_x:
        xp_ref, xs_ref, *rest = rest
        x = jnp.where(_is_sample_tile(), xs_ref[...], xp_ref[...])
    else:
        x_ref, *rest = rest
        x = x_ref[...]
    (mp_g1, ms_g1, mp_sh, ms_sh, mp_sc, ms_sc,
     wmix_ref, bmix_ref, wa_ref, wb_ref, wo_ref, npost_ref, npre_ref, *rest) = rest
    if with_router:
        wr_hi_ref, wr_lo_ref, br_ref, x1_ref, h2_ref, route_ref, route_t_ref, cnt_ref, a_scr = rest
    else:
        x1_ref, h2_ref, a_scr = rest
    for c in range(TM // GM_CHUNK):
        rows = slice(c * GM_CHUNK, (c + 1) * GM_CHUNK)
        for gi in range(GM_GROUPS):
            cols = slice(gi * GM_GROUP_DIM, (gi + 1) * GM_GROUP_DIM)
            mixed = _dot(wmix_ref[gi], v_ref[rows, cols])
            bias = bmix_ref[gi]
            mixed = mixed + jnp.concatenate([bias] * (GM_GROUP_DIM // LANES), axis=1)
            a_scr[rows, cols] = (u_ref[rows, cols].astype(_F32) * mixed).astype(_BF)
    br_a = _dot(a_scr[...], wa_ref[...])
    o = jnp.where(_is_sample_tile(), os_ref[...], op_ref[...])
    br_b = _dot(o, wb_ref[...])
    merged = sga_ref[...].astype(_F32) * br_a + sgb_ref[...].astype(_F32) * br_b
    mix = _dot(merged.astype(_BF), wo_ref[...])
    g1 = _mod_rows(mp_g1, ms_g1)
    nm = _rms(mix, npost_ref[...])
    x1 = x + _per_seq(nm, lambda t: t * g1[:, None, :])
    x1_ref[...] = x1
    h2 = _modulated_norm(x1, npre_ref, mp_sh, ms_sh, mp_sc, ms_sc)
    h2_bf = h2.astype(_BF)
    h2_ref[...] = h2_bf
    if with_router:
        _route(h2, h2_bf, wr_hi_ref, wr_lo_ref, br_ref, route_ref, route_t_ref, cnt_ref)


def _mix_call(layer, moe_idx, u, v, sga, sgb, o_p, o_s, x, modp, mods, wmix, bmix,
              wa, wb, wo, npost, npre, wr_hi=None, wr_lo=None, br=None):
    with_router = moe_idx is not None
    split_x = isinstance(x, tuple)
    ty = lambda i: (i >= NT_P).astype(_I32)
    wspec = pl.BlockSpec((None, D_MODEL, D_MODEL), lambda i: (layer, 0, 0))
    in_specs = [_tile_spec(), _tile_spec(), _tile_spec(), _tile_spec()] + _split_specs(HG_WIDTH)
    in_specs += _split_specs() if split_x else [_tile_spec()]
    in_specs += _mod_specs(layer, 2) + _mod_specs(layer, 3) + _mod_specs(layer, 4)
    in_specs += [pl.BlockSpec((None, None, GM_GROUPS, GM_CHUNK, GM_CHUNK), lambda i: (layer, ty(i), 0, 0, 0)),
                 pl.BlockSpec((None, None, GM_GROUPS, GM_CHUNK, LANES), lambda i: (layer, ty(i), 0, 0, 0)),
                 wspec, wspec, wspec, _row_spec(layer), _row_spec(layer)]
    args = [u, v, sga, sgb, o_p, o_s] + (list(x) if split_x else [x])
    args += [modp, mods, modp, mods, modp, mods, wmix, bmix, wa, wb, wo, npost, npre]
    out_specs = [_tile_spec(), _tile_spec()]
    out_shape = [jax.ShapeDtypeStruct((N_TOK, D_MODEL), _F32),
                 jax.ShapeDtypeStruct((N_TOK, D_MODEL), _BF)]
    if with_router:
        rspec = pl.BlockSpec((None, D_MODEL, LANES), lambda i: (moe_idx, 0, 0))
        in_specs += [rspec, rspec, pl.BlockSpec((None, 1, LANES), lambda i: (moe_idx, 0, 0))]
        args += [wr_hi, wr_lo, br]
        out_specs += [_tile_spec(LANES),
                      pl.BlockSpec((None, SUBLANES, TM), lambda i: (i, 0, 0)),
                      pl.BlockSpec((None, SUBLANES, LANES), lambda i: (i, 0, 0))]
        out_shape += [jax.ShapeDtypeStruct((N_TOK, LANES), _F32),
                      jax.ShapeDtypeStruct((NT, SUBLANES, TM), _F32),
                      jax.ShapeDtypeStruct((NT, SUBLANES, LANES), _I32)]
    return pl.pallas_call(
        functools.partial(_mix_kernel, with_router, split_x),
        grid=(NT,),
        in_specs=in_specs,
        out_specs=out_specs,
        out_shape=out_shape,
        scratch_shapes=[pltpu.VMEM((TM, GM_WIDTH), _BF)],
        compiler_params=_cparams(1),
        name="mix",
    )(*args)


def _swiglu(h_ref, wg_ref, wu_ref, wd_ref):
    outs = []
    for rb in range(h_ref.shape[0] // FFN_ROWS):
        h = h_ref[rb * FFN_ROWS:(rb + 1) * FFN_ROWS, :]
        gate = _dot(h, wg_ref[...])
        up = _dot(h, wu_ref[...])
        act = (_silu(gate) * up).astype(_BF)
        outs.append(_dot(act, wd_ref[...]))
    return jnp.concatenate(outs, axis=0)


def _layer_out(last, x1_ref, f, mp_g2, ms_g2, npost_ref, tail_refs):
    g2 = _mod_rows(mp_g2, ms_g2)
    nf = _rms(f, npost_ref[...])
    x2 = x1_ref[...] + _per_seq(nf, lambda t: t * g2[:, None, :])
    if last:
        yp_ref, ys_ref = tail_refs

        @pl.when(_is_sample_tile())
        def _():
            ys_ref[...] = x2

        @pl.when(jnp.logical_not(_is_sample_tile()))
        def _():
            yp_ref[...] = x2
    else:
        mp_sh, ms_sh, mp_sc, ms_sc, nw_ref, x_ref, h_ref = tail_refs
        x_ref[...] = x2
        h_ref[...] = _modulated_norm(x2, nw_ref, mp_sh, ms_sh, mp_sc, ms_sc).astype(_BF)


def _layer_out_specs(layer, last, modp, mods, npre_mix):
    if last:
        return [], [], _split_specs(), [jax.ShapeDtypeStruct((N_PROMPT, D_MODEL), _F32),
                                        jax.ShapeDtypeStruct((N_SAMPLE, D_MODEL), _F32)]
    in_specs = _mod_specs(layer + 1, 0) + _mod_specs(layer + 1, 1) + [_row_spec(layer + 1)]
    return (in_specs, [modp, mods, modp, mods, npre_mix], [_tile_spec(), _tile_spec()],
            [jax.ShapeDtypeStruct((N_TOK, D_MODEL), _F32), jax.ShapeDtypeStruct((N_TOK, D_MODEL), _BF)])


def _ffn_kernel(last, h_ref, x1_ref, mp_g2, ms_g2, wg_ref, wu_ref, wd_ref, npost_ref, *tail_refs):
    f = _swiglu(h_ref, wg_ref, wu_ref, wd_ref)
    _layer_out(last, x1_ref, f, mp_g2, ms_g2, npost_ref, tail_refs)


def _ffn_call(layer, j, h2, x1, modp, mods, wg, wu, wd, npost, npre_mix, rider=None):
    last = layer == DEPTH - 1
    extra_specs, extra_args, out_specs, out_shape = _layer_out_specs(layer, last, modp, mods, npre_mix)
    w_in = pl.BlockSpec((None, D_MODEL, D_FF), lambda i: (j, 0, 0), pipeline_mode=pl.Buffered(1))
    w_out = pl.BlockSpec((None, D_FF, D_MODEL), lambda i: (j, 0, 0), pipeline_mode=pl.Buffered(1))
    in_specs = [_tile_spec(), _tile_spec()] + _mod_specs(layer, 5) + [
        w_in, w_in, w_out, _row_spec(layer)] + extra_specs
    args = [h2, x1, modp, mods, wg, wu, wd, npost, *extra_args]
    body = _add_rider(rider, functools.partial(_ffn_kernel, last), len(args),
                      in_specs, args, out_specs, out_shape)
    return pl.pallas_call(
        body,
        grid=(NT,),
        in_specs=in_specs,
        out_specs=out_specs,
        out_shape=out_shape,
        compiler_params=_cparams(1),
        name="ffn",
    )(*args)


def _route_tables(cnt):
    cnt = cnt[:, 0, :N_EXPERTS]
    n = (cnt + (RUN_ALIGN - 1)) // RUN_ALIGN * RUN_ALIGN
    s = jnp.cumsum(n, axis=1) - n
    rows_e = jnp.sum(n, axis=0)
    region = (rows_e + (TG - 1)) // TG * TG
    region_end = jnp.cumsum(region)
    off = region_end - region
    p = off[None, :] + jnp.cumsum(n, axis=0) - n
    n_tiles = region_end[-1] // TG
    tile_row0 = jnp.minimum(jnp.arange(R_TILES, dtype=_I32), n_tiles - 1) * TG
    tile_expert = jnp.sum((tile_row0[:, None] >= region_end[None, :]).astype(_I32), axis=1)
    flat = lambda a: a.reshape(-1).astype(_I32)
    return dict(p=flat(p), s=flat(s), n=flat(n), tail_p=flat(off + rows_e), tail_n=flat(region - rows_e),
                tile_expert=flat(tile_expert), n_tiles=flat(n_tiles))


def _for_each_run_piece(n, src0, dst0, fn):
    for size in RUN_SIZES:
        done = n & (-2 * size)

        @pl.when((n & size) != 0)
        def _():
            fn(pl.multiple_of(src0 + done, RUN_ALIGN), pl.multiple_of(dst0 + done, RUN_ALIGN), size)


def _dispatch_kernel(p_tab, s_tab, n_tab, tail_p, tail_n, nt_ref, h_ref, rt_ref, xs_ref,
                     comp_scr, zero_scr, sem):
    t = pl.program_id(0)
    pos1 = rt_ref[0:1, :].astype(_I32)
    pos2 = rt_ref[1:2, :].astype(_I32)
    r = lax.broadcasted_iota(_I32, (CB, TM), 0)
    perm = jnp.where(jnp.logical_or(r == pos1, r == pos2), 1.0, 0.0).astype(_BF)
    slot = jnp.bitwise_and(t, 1)
    comp_scr[slot] = _dot(perm, h_ref[...]).astype(_BF)

    def runs(tile, buf, go):
        for e in range(N_EXPERTS):
            idx = tile * N_EXPERTS + e

            def piece(src, dst, size):
                go(pltpu.make_async_copy(comp_scr.at[buf, pl.ds(src, size)], xs_ref.at[pl.ds(dst, size)],
                                         sem.at[buf]))

            _for_each_run_piece(n_tab[idx], s_tab[idx], p_tab[idx], piece)

    runs(t, slot, lambda cp: cp.start())

    @pl.when(t > 0)
    def _():
        runs(t - 1, 1 - slot, lambda cp: cp.wait())

    @pl.when(t == NT - 1)
    def _():
        runs(t, slot, lambda cp: cp.wait())
        zero_scr[...] = jnp.zeros_like(zero_scr)
        zsem = sem.at[0]

        def tails(go):
            for e in range(N_EXPERTS):
                def piece(src, dst, size):
                    go(pltpu.make_async_copy(zero_scr.at[pl.ds(0, size)], xs_ref.at[pl.ds(dst, size)], zsem))

                _for_each_run_piece(tail_n[e], 0, tail_p[e], piece)

        tails(lambda cp: cp.start())
        tails(lambda cp: cp.wait())

        def spare_tile(go):
            def body(r, carry):
                dst = pl.multiple_of(r * TG, TG)
                go(pltpu.make_async_copy(zero_scr, xs_ref.at[pl.ds(dst, TG)], zsem))
                return carry
            lax.fori_loop(nt_ref[0], R_TILES, body, 0)

        spare_tile(lambda cp: cp.start())
        spare_tile(lambda cp: cp.wait())


def _dispatch_call(tabs, h2, route_t):
    grid_spec = pltpu.PrefetchScalarGridSpec(
        num_scalar_prefetch=6,
        grid=(NT,),
        in_specs=[_tile_spec(), pl.BlockSpec((None, SUBLANES, TM), lambda i, *_: (i, 0, 0))],
        out_specs=pl.BlockSpec(memory_space=pl.ANY),
        scratch_shapes=[pltpu.VMEM((2, CB, D_MODEL), _BF), pltpu.VMEM((TG, D_MODEL), _BF),
                        pltpu.SemaphoreType.DMA((2,))])
    return pl.pallas_call(
        _dispatch_kernel,
        grid_spec=grid_spec,
        out_shape=jax.ShapeDtypeStruct((N_ROWS, D_MODEL), _BF),
        compiler_params=_cparams(1),
        name="dispatch",
    )(tabs["p"], tabs["s"], tabs["n"], tabs["tail_p"], tabs["tail_n"], tabs["n_tiles"], h2, route_t)


def _expert_kernel(te_ref, nt_ref, x_ref, wg_ref, wu_ref, wd_ref, y_ref):
    r = pl.program_id(0)

    @pl.when(r < nt_ref[0])
    def _():
        y_ref[...] = _swiglu(x_ref, wg_ref, wu_ref, wd_ref).astype(_BF)

    @pl.when(r >= nt_ref[0])
    def _():
        y_ref[...] = jnp.zeros_like(y_ref)


def _expert_call(tabs, xs, wg, wu, wd):
    wg, wu = wg.reshape(N_EXPERTS, D_MODEL, D_FF), wu.reshape(N_EXPERTS, D_MODEL, D_FF)
    wd = wd.reshape(N_EXPERTS, D_FF, D_MODEL)
    w_in = pl.BlockSpec((None, D_MODEL, D_FF), lambda r, te, nt: (te[r], 0, 0))
    w_out = pl.BlockSpec((None, D_FF, D_MODEL), lambda r, te, nt: (te[r], 0, 0))
    grid_spec = pltpu.PrefetchScalarGridSpec(
        num_scalar_prefetch=2,
        grid=(R_TILES,),
        in_specs=[pl.BlockSpec((TG, D_MODEL), lambda r, te, nt: (jnp.minimum(r, nt[0] - 1), 0)),
                  w_in, w_in, w_out],
        out_specs=pl.BlockSpec((TG, D_MODEL), lambda r, te, nt: (r, 0)))
    return pl.pallas_call(
        _expert_kernel,
        grid_spec=grid_spec,
        out_shape=jax.ShapeDtypeStruct((N_ROWS, D_MODEL), _BF),
        compiler_params=_cparams(1),
        name="experts",
    )(tabs["tile_expert"], tabs["n_tiles"], xs, wg, wu, wd)


def _combine_kernel(last, p_tab, s_tab, n_tab, route_ref, x1_ref, mp_g2, ms_g2, npost_ref, ys_ref, *rest):
    *tail_refs, yc_scr, sem = rest
    t = pl.program_id(0)
    slot = jnp.bitwise_and(t, 1)

    def runs(tile, buf, go):
        for e in range(N_EXPERTS):
            idx = tile * N_EXPERTS + e

            def piece(src, dst, size):
                go(pltpu.make_async_copy(ys_ref.at[pl.ds(dst, size)], yc_scr.at[buf, pl.ds(src, size)],
                                         sem.at[buf]))

            _for_each_run_piece(n_tab[idx], s_tab[idx], p_tab[idx], piece)

    @pl.when(t == 0)
    def _():
        yc_scr[...] = jnp.zeros_like(yc_scr)
        runs(t, slot, lambda cp: cp.start())

    @pl.when(t + 1 < NT)
    def _():
        runs(t + 1, 1 - slot, lambda cp: cp.start())

    runs(t, slot, lambda cp: cp.wait())
    route = route_ref[...]
    pos1 = route[:, 0:1].astype(_I32)
    pos2 = route[:, 1:2].astype(_I32)
    w1 = route[:, 2:3]
    w2 = route[:, 3:4]
    sorted_row = lax.broadcasted_iota(_I32, (TM, CB), 1)
    yc = yc_scr[slot]
    pick1 = jnp.where(sorted_row == pos1, 1.0, 0.0).astype(_BF)
    pick2 = jnp.where(sorted_row == pos2, 1.0, 0.0).astype(_BF)
    f = w1 * _dot(pick1, yc) + w2 * _dot(pick2, yc)
    _layer_out(last, x1_ref, f, mp_g2, ms_g2, npost_ref, tail_refs)


def _combine_call(layer, tabs, route, x1, modp, mods, npost, ys, npre_mix):
    last = layer == DEPTH - 1
    extra_specs, extra_args, out_specs, out_shape = _layer_out_specs(layer, last, modp, mods, npre_mix)
    grid_spec = pltpu.PrefetchScalarGridSpec(
        num_scalar_prefetch=3,
        grid=(NT,),
        in_specs=[_tile_spec(LANES), _tile_spec()] + _mod_specs(layer, 5) + [
            _row_spec(layer), pl.BlockSpec(memory_space=pl.ANY)] + extra_specs,
        out_specs=out_specs,
        scratch_shapes=[pltpu.VMEM((2, CB, D_MODEL), _BF), pltpu.SemaphoreType.DMA((2,))])
    return pl.pallas_call(
        functools.partial(_combine_kernel, last),
        grid_spec=grid_spec,
        out_shape=out_shape,
        compiler_params=_cparams(1),
        name="combine",
    )(tabs["p"], tabs["s"], tabs["n"], route, x1, modp, mods, npost, ys, *extra_args)


def _spatial_tables(gm_ws, gm_bs):
    mask_p = jnp.tril(jnp.ones((GM_CHUNK, GM_CHUNK), bool))
    w_p = jnp.where(mask_p, gm_ws, 0.0)
    mask_s = jnp.tril(jnp.ones((DEC_SEQ, DEC_SEQ), bool))
    w_small = jnp.where(mask_s, gm_ws[:, :, :DEC_SEQ, :DEC_SEQ], 0.0)
    eye = jnp.eye(GM_CHUNK // DEC_SEQ, dtype=gm_ws.dtype)
    w_s = jnp.einsum("ab,lgts->lgatbs", eye, w_small).reshape(gm_ws.shape)
    b_p = gm_bs
    b_s = jnp.tile(gm_bs[:, :, :DEC_SEQ], (1, 1, GM_CHUNK // DEC_SEQ))
    wmix = jnp.stack([w_p, w_s], axis=1).astype(_BF)
    bmix = jnp.stack([b_p, b_s], axis=1)[..., None]
    bmix = jnp.broadcast_to(bmix, bmix.shape[:-1] + (LANES,)).astype(_F32)
    return wmix, bmix


def kernel(x_prompt, x_sample, c_prompt, c_sample, state_hgrn, w_in, gm_ln_w, gm_ln_b, gm_ws, gm_bs,
           hg_lb_raw, hg_norm_w, w_branch_a, w_branch_b, w_out, w_ada, b_ada,
           norm_pre_mix, norm_post_mix, norm_pre_ffn, norm_post_ffn,
           w_ffn_gate, w_ffn_up, w_ffn_down, w_router, b_router, w_exp_gate, w_exp_up, w_exp_down):
    x = (x_prompt.reshape(N_PROMPT, D_MODEL), x_sample.reshape(N_SAMPLE, D_MODEL))
    c_all = jnp.concatenate([c_prompt, c_sample], axis=0)
    mod = _ada_call(c_all, w_ada, b_ada)
    modp = mod[:, :BATCH].reshape(DEPTH, BATCH, 1, 6 * D_MODEL)
    mods = mod[:, BATCH:]
    loglb = _lb_call(hg_lb_raw)
    aux = jnp.zeros((DEPTH, SUBLANES, D_MODEL), _F32)
    aux = aux.at[:, 0].set(gm_ln_w).at[:, 1].set(gm_ln_b).at[:, 2].set(loglb)
    wmix, bmix = _spatial_tables(gm_ws, gm_bs)
    row3 = lambda a: a.reshape(a.shape[0], 1, a.shape[1])
    w_in_bf = w_in.astype(_BF)
    wa_bf, wb_bf, wo_bf = w_branch_a.astype(_BF), w_branch_b.astype(_BF), w_out.astype(_BF)
    wfg, wfu, wfd = w_ffn_gate.astype(_BF), w_ffn_up.astype(_BF), w_ffn_down.astype(_BF)
    wr_pad = jnp.pad(w_router, ((0, 0), (0, 0), (0, LANES - N_EXPERTS)))
    wr_hi = wr_pad.astype(_BF)
    wr_lo = (wr_pad - wr_hi.astype(_F32)).astype(_BF)
    br_pad = jnp.pad(b_router, ((0, 0), (0, LANES - N_EXPERTS))).reshape(-1, 1, LANES)
    npre_mix, npost_mix = row3(norm_pre_mix), row3(norm_post_mix)
    npre_ffn, npost_ffn = row3(norm_pre_ffn), row3(norm_post_ffn)
    hg_nw = row3(hg_norm_w)

    hp, vs = [], []
    sample_states = jnp.zeros(state_hgrn.shape, _F32)
    h = _prenorm_call(0, x[0], x[1], modp, mods, npre_mix)
    for l in range(DEPTH):
        j = l // 2
        dense = l % 2 == 0
        ride = dense and l + 1 < DEPTH
        rider = (lambda w: (w, (l + 1) // 2)) if ride else (lambda w: None)
        act0 = list(_inproj_call(l, 0, ("gelu", "gelu_ln", "silu", "logf"), h, w_in_bf, aux, TM,
                                 rider(w_exp_gate)))
        act4 = list(_inproj_call(l, 4, ("id", "silu", "sigmoid", "sigmoid"), h, w_in_bf, aux, INPROJ_ROWS,
                                 rider(w_exp_up)))
        if ride:
            expert_w = [act0.pop(), act4.pop()]
        u, v, v32, q, g = act0
        iv, sg, sga, sgb = act4
        o_p, s_p = _hgrn_prompt_call(l, q, g, iv, sg, hg_nw)
        o_s, sample_states = _hgrn_sample_call(l, q, g, iv, sg, hg_nw, state_hgrn, sample_states)
        if dense:
            x1, h2 = _mix_call(l, None, u, v, sga, sgb, o_p, o_s, x, modp, mods, wmix, bmix,
                               wa_bf, wb_bf, wo_bf, npost_mix, npre_ffn)
            out = list(_ffn_call(l, j, h2, x1, modp, mods, wfg, wfu, wfd, npost_ffn, npre_mix,
                                 rider(w_exp_down)))
            if ride:
                expert_w.append(out.pop())
        else:
            x1, h2, route, route_t, cnt = _mix_call(l, j, u, v, sga, sgb, o_p, o_s, x, modp, mods, wmix, bmix,
                                                    wa_bf, wb_bf, wo_bf, npost_mix, npre_ffn,
                                                    wr_hi, wr_lo, br_pad)
            tabs = _route_tables(cnt)
            xs = _dispatch_call(tabs, h2, route_t)
            ys = _expert_call(tabs, xs, *expert_w)
            out = _combine_call(l, tabs, route, x1, modp, mods, npost_ffn, ys, npre_mix)
        if l < DEPTH - 1:
            x, h = out
        hp.append(s_p)
        vs.append(v32.reshape(DEC_BATCH, DEC_SEQ, GM_WIDTH))
    y_prompt = out[0].reshape(BATCH, SEQ, D_MODEL)
    y_sample = out[1].reshape(DEC_BATCH, DEC_SEQ, D_MODEL)
    return (y_prompt, y_sample, jnp.stack(hp, axis=0), sample_states, jnp.stack(vs, axis=0))
```

```python
import functools
import math

import jax
import jax.numpy as jnp
from jax import lax
from jax.experimental import pallas as pl
from jax.experimental.pallas import tpu as pltpu

D_MODEL = 1024
BATCH = 8
SEQ = 2048
DEPTH = 4
DEC_BATCH = 128
DEC_SEQ = 8
GM_WIDTH = 1024
GM_GROUPS = 4
GM_GROUP_DIM = GM_WIDTH // GM_GROUPS
GM_CHUNK = 128
HG_HEADS = 8
HG_DK = 128
HG_DV = 128
HG_WIDTH = HG_HEADS * HG_DK
D_FF = 2816
N_EXPERTS = 8
TOP_K = 2
IN_COLS = 2 * GM_WIDTH + 4 * HG_WIDTH + 2 * D_MODEL
EPS = 1e-6

LANES = 128
SUBLANES = 8
BF16_ROWS = 16
TM = 512
N_PROMPT = BATCH * SEQ
N_SAMPLE = DEC_BATCH * DEC_SEQ
N_TOK = N_PROMPT + N_SAMPLE
NT_P = N_PROMPT // TM
NT_S = N_SAMPLE // TM
NT = NT_P + NT_S
TILES_PER_SEQ = SEQ // TM
SEQ_PER_TILE = TM // DEC_SEQ
HG_C = 64
HG_BLK = 256
HG_SB = 8
FFN_ROWS = 256
MIX_ROWS = 256
INPROJ_ROWS_A = 512
INPROJ_ROWS_B = 256
RIDER_BLOCKS = 32
EXP_RANGE = 70.0
VMEM_LIMIT = 56 * 1024 * 1024

TG = 512
RUN_ALIGN = BF16_ROWS
RUN_SIZES = tuple(TM >> k for k in range((TM // RUN_ALIGN).bit_length()))
CB = TOP_K * TM + N_EXPERTS * RUN_ALIGN
_ROWS_BOUND = TOP_K * N_TOK + NT * N_EXPERTS * (RUN_ALIGN - 1) + N_EXPERTS * (TG - RUN_ALIGN)
R_TILES = -(-_ROWS_BOUND // TG)
N_ROWS = R_TILES * TG

_BF = jnp.bfloat16
_F32 = jnp.float32
_I32 = jnp.int32


def _cparams(n_axes):
    return pltpu.CompilerParams(dimension_semantics=("arbitrary",) * n_axes,
                                vmem_limit_bytes=VMEM_LIMIT)


def _dot(a, b):
    return jnp.dot(a, b, preferred_element_type=_F32)


def _sigmoid(x):
    return 0.5 + 0.5 * jnp.tanh(0.5 * x)


def _silu(x):
    hx = 0.5 * x
    return hx + hx * jnp.tanh(hx)


def _gelu(x):
    return 0.5 * x * (1.0 + lax.erf(x * (1.0 / math.sqrt(2.0))))


def _rms(x, w):
    return x * lax.rsqrt(jnp.mean(x * x, axis=-1, keepdims=True) + EPS) * w


def _is_sample_tile():
    return pl.program_id(0) >= NT_P


def _mod_rows(mp_ref, ms_ref):
    mp = jnp.broadcast_to(mp_ref[...], ms_ref.shape)
    return jnp.where(_is_sample_tile(), ms_ref[...], mp)


def _per_seq(x, fn_rows):
    x3 = x.reshape(x.shape[0] // DEC_SEQ, DEC_SEQ, x.shape[-1])
    return fn_rows(x3).reshape(x.shape)


def _mod_specs(layer, col_block):
    mp = pl.BlockSpec((None, None, 1, D_MODEL),
                      lambda i, *_: (layer, jnp.minimum(i // TILES_PER_SEQ, BATCH - 1), 0, col_block))
    ms = pl.BlockSpec((None, SEQ_PER_TILE, D_MODEL),
                      lambda i, *_: (layer, jnp.maximum(i - NT_P, 0), col_block))
    return [mp, ms]


def _tile_spec(width=D_MODEL):
    return pl.BlockSpec((TM, width), lambda i, *_: (i, 0))


def _row_spec(layer, width=D_MODEL):
    return pl.BlockSpec((None, 1, width), lambda i, *_: (layer, 0, 0))


def _ada_kernel(c_ref, w_ref, b_ref, o_ref):
    s = _silu(c_ref[...]).astype(_BF)
    o_ref[...] = _dot(s, w_ref[...].astype(_BF)) + b_ref[...]


def _ada_call(c_all, w_ada, b_ada):
    nb = 4
    wcol = 6 * D_MODEL // nb
    rows = c_all.shape[0]
    return pl.pallas_call(
        _ada_kernel,
        grid=(DEPTH, nb),
        in_specs=[pl.BlockSpec((rows, D_MODEL), lambda l, j: (0, 0)),
                  pl.BlockSpec((None, D_MODEL, wcol), lambda l, j: (l, 0, j)),
                  pl.BlockSpec((None, 1, wcol), lambda l, j: (l, 0, j))],
        out_specs=pl.BlockSpec((None, rows, wcol), lambda l, j: (l, 0, j)),
        out_shape=jax.ShapeDtypeStruct((DEPTH, rows, 6 * D_MODEL), _F32),
        compiler_params=_cparams(2),
        name="ada",
    )(c_all, w_ada, b_ada.reshape(DEPTH, 1, 6 * D_MODEL))


def _lb_kernel(raw_ref, loglb_ref):
    raw = raw_ref[...]
    m = jnp.max(raw, axis=0, keepdims=True)
    e = jnp.exp(raw - m)
    p = e / jnp.sum(e, axis=0, keepdims=True)
    acc = jnp.zeros((1, HG_WIDTH), _F32)
    for l in range(DEPTH):
        if l > 0:
            acc = acc + p[l:l + 1, :]
        loglb_ref[l:l + 1, :] = jnp.log(acc)


def _lb_call(raw):
    return pl.pallas_call(
        _lb_kernel,
        out_shape=jax.ShapeDtypeStruct((DEPTH, HG_WIDTH), _F32),
        name="lower_bounds",
    )(raw)


def _scale_shift_norm(x, nw, sh, sc):
    xn = _rms(x, nw)
    return _per_seq(xn, lambda x3: x3 * (1.0 + sc)[:, None, :] + sh[:, None, :])


def _modulated_norm(x, nw_ref, mp_sh, ms_sh, mp_sc, ms_sc):
    return _scale_shift_norm(x, nw_ref[...], _mod_rows(mp_sh, ms_sh), _mod_rows(mp_sc, ms_sc))


def _prenorm_kernel(xp_ref, xs_ref, mp_sh, ms_sh, mp_sc, ms_sc, nw_ref, h_ref):
    x = jnp.where(_is_sample_tile(), xs_ref[...], xp_ref[...])
    h_ref[...] = _modulated_norm(x, nw_ref, mp_sh, ms_sh, mp_sc, ms_sc).astype(_BF)


def _split_specs(width=D_MODEL):
    return [pl.BlockSpec((TM, width), lambda i, *_: (jnp.minimum(i, NT_P - 1), 0)),
            pl.BlockSpec((TM, width), lambda i, *_: (jnp.maximum(i - NT_P, 0), 0))]


def _prenorm_call(layer, x_p, x_s, modp, mods, norm_w):
    return pl.pallas_call(
        _prenorm_kernel,
        grid=(NT,),
        in_specs=_split_specs() + _mod_specs(layer, 0) + _mod_specs(layer, 1) + [_row_spec(layer)],
        out_specs=_tile_spec(),
        out_shape=jax.ShapeDtypeStruct((N_TOK, D_MODEL), _BF),
        compiler_params=_cparams(1),
        name="prenorm",
    )(x_p, x_s, modp, mods, modp, mods, norm_w)


def _with_cast_rider(body, n_in):
    def wrapped(*refs):
        rider_in, rider_out = refs[n_in], refs[-1]
        rider_out[...] = rider_in[...].astype(_BF)
        body(*refs[:n_in], *refs[n_in + 1:-1])
    return wrapped


def _add_rider(rider, body, n_in, in_specs, args, out_specs, out_shape):
    if rider is None:
        return body
    w, j = rider
    w2 = w.reshape(-1, w.shape[-1])
    rows_per_layer = w2.shape[0] // w.shape[0]
    rows = rows_per_layer // RIDER_BLOCKS
    blk = lambda i: jnp.minimum(i, RIDER_BLOCKS - 1)
    in_specs.append(pl.BlockSpec((rows, w2.shape[1]), lambda i, *_: (j * RIDER_BLOCKS + blk(i), 0)))
    args.append(w2)
    out_specs.append(pl.BlockSpec((rows, w2.shape[1]), lambda i, *_: (blk(i), 0)))
    out_shape.append(jax.ShapeDtypeStruct((rows_per_layer, w2.shape[1]), _BF))
    return _with_cast_rider(body, n_in)


def _inproj_kernel(kinds, row_block, h_ref, aux_ref, *refs):
    w_refs, out_refs = refs[:len(kinds)], refs[len(kinds):]
    for rb in range(TM // row_block):
        rs = slice(rb * row_block, (rb + 1) * row_block)
        h = h_ref[rs, :]
        outs = list(out_refs)
        for j, kind in enumerate(kinds):
            z = _dot(h, w_refs[j][...])
            if kind == "gelu":
                outs.pop(0)[rs, :] = _gelu(z).astype(_BF)
            elif kind == "gelu_ln":
                a = _gelu(z)
                mu = jnp.mean(a, axis=-1, keepdims=True)
                ac = a - mu
                vn = ac * lax.rsqrt(jnp.mean(ac * ac, axis=-1, keepdims=True) + EPS)
                vn = vn * aux_ref[0:1, :] + aux_ref[1:2, :]
                outs.pop(0)[rs, :] = vn.astype(_BF)
                v32_ref = outs.pop(0)

                @pl.when(_is_sample_tile())
                def _():
                    v32_ref[rs, :] = vn
            elif kind == "silu":
                outs.pop(0)[rs, :] = _silu(z).astype(_BF)
            elif kind == "sigmoid":
                outs.pop(0)[rs, :] = _sigmoid(z).astype(_BF)
            elif kind == "id":
                outs.pop(0)[rs, :] = z.astype(_BF)
            elif kind == "logf":
                loglb = aux_ref[2:3, :]
                tail = jnp.log(1.0 + jnp.exp(-jnp.abs(z)))
                a = jnp.minimum(z, 0.0) - tail
                c = loglb - jnp.maximum(z, 0.0) - tail
                outs.pop(0)[rs, :] = jnp.maximum(a, c) + jnp.log(1.0 + jnp.exp(-jnp.abs(z - loglb)))
            else:
                raise ValueError(kind)


def _inproj_call(layer, sections, h, w_in_bf, aux, row_block, rider=None):
    kinds = tuple(kind for _, kind in sections)
    out_specs, out_shape = [], []
    for kind in kinds:
        dt = _F32 if kind == "logf" else _BF
        out_specs.append(_tile_spec())
        out_shape.append(jax.ShapeDtypeStruct((N_TOK, D_MODEL), dt))
        if kind == "gelu_ln":
            out_specs.append(pl.BlockSpec((TM, D_MODEL), lambda i: (jnp.maximum(i - NT_P, 0), 0)))
            out_shape.append(jax.ShapeDtypeStruct((N_SAMPLE, D_MODEL), _F32))
    in_specs = [_tile_spec(), pl.BlockSpec((None, SUBLANES, D_MODEL), lambda i: (layer, 0, 0))]
    in_specs += [pl.BlockSpec((None, D_MODEL, D_MODEL), lambda i, sec=sec: (layer, 0, sec)) for sec, _ in sections]
    args = [h, aux] + [w_in_bf] * len(sections)
    body = _add_rider(rider, functools.partial(_inproj_kernel, kinds, row_block), len(args),
                      in_specs, args, out_specs, out_shape)
    return pl.pallas_call(
        body,
        grid=(NT,),
        in_specs=in_specs,
        out_specs=out_specs,
        out_shape=out_shape,
        compiler_params=_cparams(1),
        name=f"inproj_{sections[0][0]}",
    )(*args)


def _col_bcast(row):
    n = row.shape[-1]
    return jnp.broadcast_to(row, (n, n)).T


def _head_out(o, nw, sg):
    return (_rms(o, nw) * sg.astype(_F32)).astype(_BF)


def _hgrn_exact(ns, rows, q_ref, g_ref, i_ref, sg_ref, nw_ref, o_ref, s_get, s_put):
    n = ns * DEC_SEQ
    shape3 = (ns, DEC_SEQ, HG_WIDTH)
    t_row = jnp.bitwise_and(lax.broadcasted_iota(_I32, (n, HG_WIDTH), 0), DEC_SEQ - 1)
    t3 = lax.broadcasted_iota(_I32, (ns, DEC_SEQ, 1), 1)
    g = g_ref[rows, :]
    b = g
    shift = 1
    while shift < DEC_SEQ:
        b = b + jnp.where(t_row >= shift, pltpu.roll(b, shift, axis=0), 0.0)
        shift *= 2
    kk = 1.0 - jnp.exp(g)
    q = q_ref[rows, :].astype(_F32)
    b3 = b.reshape(shape3)
    bend3 = b3[:, DEC_SEQ - 1:DEC_SEQ, :]
    qe = (q * jnp.exp(b)).astype(_BF)
    kdec = (kk.reshape(shape3) * jnp.exp(bend3 - b3)).reshape(n, HG_WIDTH).astype(_BF)
    e_end = jnp.exp(bend3)
    iv_all = i_ref[rows, :]
    sg_all = sg_ref[rows, :]
    q3, k3, i3 = q.reshape(shape3), kk.reshape(shape3), iv_all.astype(_F32).reshape(shape3)
    for h in range(HG_HEADS):
        hs = slice(h * HG_DK, (h + 1) * HG_DK)
        qh, bh, kh, ih = q3[:, :, hs], b3[:, :, hs], k3[:, :, hs], i3[:, :, hs]
        intra = jnp.zeros((ns, DEC_SEQ, HG_DV), _F32)
        for j in range(DEC_SEQ):
            decay = jnp.exp(jnp.minimum(bh - bh[:, j:j + 1, :], 0.0))
            w = jnp.sum(qh * decay * kh[:, j:j + 1, :], axis=-1, keepdims=True)
            intra = intra + jnp.where(t3 >= j, w, 0.0) * ih[:, j:j + 1, :]
        outs = []
        for si in range(ns):
            rs = slice(si * DEC_SEQ, (si + 1) * DEC_SEQ)
            s_old = s_get(si, h)
            outs.append(intra[si] + _dot(qe[rs, hs], s_old.astype(_BF)))
            upd = lax.dot_general(kdec[rs, hs], iv_all[rs, hs], (((0,), (0,)), ((), ())),
                                  preferred_element_type=_F32)
            s_put(si, h, _col_bcast(e_end[si, :, hs]) * s_old + upd)
        o = outs[0] if ns == 1 else jnp.concatenate(outs, axis=0)
        o_ref[rows, hs] = _head_out(o, nw_ref[...], sg_all[:, hs])


def _rows_of_chunks(x, row_in_chunk):
    parts = []
    for c in range(HG_BLK // HG_C):
        r = c * HG_C + row_in_chunk
        parts.append(jnp.broadcast_to(x[r:r + 1, :], (HG_C, x.shape[-1])))
    return jnp.concatenate(parts, axis=0)


def _hgrn_prompt_kernel(q_ref, g_ref, i_ref, sg_ref, nw_ref, o_ref, s_out_ref, s_scr):
    c = pl.program_id(1)

    @pl.when(c == 0)
    def _():
        s_scr[...] = jnp.zeros_like(s_scr)

    g = g_ref[...]
    row = lax.broadcasted_iota(_I32, (HG_BLK, HG_BLK), 0)
    col = lax.broadcasted_iota(_I32, (HG_BLK, HG_BLK), 1)
    tril = jnp.logical_and(row >= col, row // HG_C == col // HG_C).astype(_BF)
    g_hi = g.astype(_BF)
    r1 = g - g_hi.astype(_F32)
    g_mid = r1.astype(_BF)
    g_lo = (r1 - g_mid.astype(_F32)).astype(_BF)
    b = _dot(tril, g_hi) + _dot(tril, g_mid) + _dot(tril, g_lo)
    ref = _rows_of_chunks(b, HG_C // 2 - 1)
    in_range = jnp.max(jnp.abs(b - ref)) <= EXP_RANGE

    @pl.when(in_range)
    def _():
        kk = 1.0 - jnp.exp(g)
        q = q_ref[...].astype(_F32)
        bend = _rows_of_chunks(b, HG_C - 1)
        qt = (q * jnp.exp(b - ref)).astype(_BF)
        kt = (kk * jnp.exp(ref - b)).astype(_BF)
        qe = (q * jnp.exp(b)).astype(_BF)
        kdec = (kk * jnp.exp(bend - b)).astype(_BF)
        e_end = [jnp.exp(b[(ci + 1) * HG_C - 1:(ci + 1) * HG_C, :]) for ci in range(HG_BLK // HG_C)]
        crow = lax.broadcasted_iota(_I32, (HG_C, HG_C), 0)
        ccol = lax.broadcasted_iota(_I32, (HG_C, HG_C), 1)
        causal = crow >= ccol
        for h in range(HG_HEADS):
            hs = slice(h * HG_DK, (h + 1) * HG_DK)
            s_cur = s_scr[h]
            for ci in range(HG_BLK // HG_C):
                rs = slice(ci * HG_C, (ci + 1) * HG_C)
                iv = i_ref[rs, hs]
                scores = lax.dot_general(qt[rs, hs], kt[rs, hs], (((1,), (1,)), ((), ())),
                                         preferred_element_type=_F32)
                scores = jnp.where(causal, scores, 0.0).astype(_BF)
                o = _dot(scores, iv) + _dot(qe[rs, hs], s_cur.astype(_BF))
                upd = lax.dot_general(kdec[rs, hs], iv, (((0,), (0,)), ((), ())),
                                      preferred_element_type=_F32)
                s_cur = _col_bcast(e_end[ci][:, hs]) * s_cur + upd
                o_ref[rs, hs] = _head_out(o, nw_ref[...], sg_ref[rs, hs])
            s_scr[h] = s_cur

    @pl.when(jnp.logical_not(in_range))
    def _():
        def s_put(si, h, val):
            s_scr[h] = val

        def sub_chunk(k, carry):
            rows = pl.ds(pl.multiple_of(k * DEC_SEQ, DEC_SEQ), DEC_SEQ)
            _hgrn_exact(1, rows, q_ref, g_ref, i_ref, sg_ref, nw_ref, o_ref, lambda si, h: s_scr[h], s_put)
            return carry

        lax.fori_loop(0, HG_BLK // DEC_SEQ, sub_chunk, 0)

    @pl.when(c == pl.num_programs(1) - 1)
    def _():
        s_out_ref[...] = s_scr[...]


def _hgrn_prompt_call(layer, q, g, iv, sg, hg_norm_w):
    nc = SEQ // HG_BLK
    blk = pl.BlockSpec((HG_BLK, HG_WIDTH), lambda b, c: (b * nc + c, 0))
    return pl.pallas_call(
        _hgrn_prompt_kernel,
        grid=(BATCH, nc),
        in_specs=[blk, blk, blk, blk,
                  pl.BlockSpec((None, 1, HG_DV), lambda b, c: (layer, 0, 0))],
        out_specs=[blk,
                   pl.BlockSpec((None, HG_HEADS, HG_DK, HG_DV), lambda b, c: (b, 0, 0, 0))],
        out_shape=[jax.ShapeDtypeStruct((N_PROMPT, HG_WIDTH), _BF),
                   jax.ShapeDtypeStruct((BATCH, HG_HEADS, HG_DK, HG_DV), _F32)],
        scratch_shapes=[pltpu.VMEM((HG_HEADS, HG_DK, HG_DV), _F32)],
        compiler_params=_cparams(2),
        name="hgrn_prompt",
    )(q, g, iv, sg, hg_norm_w)


def _hgrn_sample_kernel(q_ref, g_ref, i_ref, sg_ref, nw_ref, s_in_ref, all_states_ref, o_ref, s_out_ref):
    del all_states_ref

    def s_put(si, h, val):
        s_out_ref[si, h] = val

    _hgrn_exact(HG_SB, slice(None), q_ref, g_ref, i_ref, sg_ref, nw_ref, o_ref, lambda si, h: s_in_ref[si, h], s_put)


def _hgrn_sample_call(layer, q, g, iv, sg, hg_norm_w, state, all_states):
    rows = HG_SB * DEC_SEQ
    off = N_PROMPT // rows
    blk = pl.BlockSpec((rows, HG_WIDTH), lambda j: (off + j, 0))
    s_blk = pl.BlockSpec((None, HG_SB, HG_HEADS, HG_DK, HG_DV), lambda j: (layer, j, 0, 0, 0))
    return pl.pallas_call(
        _hgrn_sample_kernel,
        grid=(DEC_BATCH // HG_SB,),
        in_specs=[blk, blk, blk, blk,
                  pl.BlockSpec((None, 1, HG_DV), lambda j: (layer, 0, 0)),
                  s_blk, pl.BlockSpec(memory_space=pl.ANY)],
        out_specs=[pl.BlockSpec((rows, HG_WIDTH), lambda j: (j, 0)), s_blk],
        out_shape=[jax.ShapeDtypeStruct((N_SAMPLE, HG_WIDTH), _BF),
                   jax.ShapeDtypeStruct(all_states.shape, _F32)],
        input_output_aliases={6: 1},
        compiler_params=_cparams(1),
        name="hgrn_sample",
    )(q, g, iv, sg, hg_norm_w, state, all_states)


def _route(h2, h2_bf, wr_hi_ref, wr_lo_ref, br_ref, route_ref, route_t_ref, cnt_ref):
    h2_lo = (h2 - h2_bf.astype(_F32)).astype(_BF)
    logits = (_dot(h2_bf, wr_hi_ref[...]) + (_dot(h2_lo, wr_hi_ref[...]) + _dot(h2_bf, wr_lo_ref[...]))
              + br_ref[...])
    lane = lax.broadcasted_iota(_I32, logits.shape, 1)
    neg = jnp.float32(-jnp.inf)
    logits = jnp.where(lane < N_EXPERTS, logits, neg)
    m1 = jnp.max(logits, axis=-1, keepdims=True)
    i1 = jnp.min(jnp.where(logits == m1, lane, LANES), axis=-1, keepdims=True)
    rest_l = jnp.where(lane == i1, neg, logits)
    m2 = jnp.max(rest_l, axis=-1, keepdims=True)
    i2 = jnp.min(jnp.where(rest_l == m2, lane, LANES), axis=-1, keepdims=True)
    e2 = jnp.exp(m2 - m1)
    w1 = 1.0 / (1.0 + e2)
    w2 = e2 / (1.0 + e2)
    hot1 = lane == i1
    hot2 = lane == i2
    tot = jnp.where(jnp.logical_or(hot1, hot2), 1.0, 0.0)
    trow = lax.broadcasted_iota(_I32, (TM, TM), 0)
    tcol = lax.broadcasted_iota(_I32, (TM, TM), 1)
    before = _dot((trow > tcol).astype(_BF), tot.astype(_BF))
    cnt = jnp.sum(tot, axis=0, keepdims=True)
    cnt_pad = jnp.ceil(cnt * (1.0 / RUN_ALIGN)) * RUN_ALIGN
    erow = lax.broadcasted_iota(_I32, (LANES, LANES), 0)
    ecol = lax.broadcasted_iota(_I32, (LANES, LANES), 1)
    start = _dot(jnp.broadcast_to(cnt_pad, (SUBLANES, LANES)).astype(_BF),
                 (erow < ecol).astype(_BF))[0:1, :]
    slot = before + start
    pos1 = jnp.sum(jnp.where(hot1, slot, 0.0), axis=-1, keepdims=True)
    pos2 = jnp.sum(jnp.where(hot2, slot, 0.0), axis=-1, keepdims=True)
    route = jnp.where(lane == 0, pos1, jnp.where(lane == 1, pos2,
                      jnp.where(lane == 2, w1, jnp.where(lane == 3, w2, 0.0))))
    route_ref[...] = route
    route_t_ref[...] = route.T[0:SUBLANES, :]
    cnt_ref[...] = jnp.broadcast_to(cnt, (SUBLANES, LANES)).astype(_I32)


def _mix_kernel(with_router, split_x, u_ref, v_ref, sga_ref, sgb_ref, op_ref, os_ref, *rest):
    if split_x:
        xp_ref, xs_ref, *rest = rest
        x = jnp.where(_is_sample_tile(), xs_ref[...], xp_ref[...])
    else:
        x_ref, *rest = rest
        x = x_ref[...]
    (mp_g1, ms_g1, mp_sh, ms_sh, mp_sc, ms_sc,
     wmix_ref, bmix_ref, wa_ref, wb_ref, wo_ref, npost_ref, npre_ref, *rest) = rest
    if with_router:
        wr_hi_ref, wr_lo_ref, br_ref, x1_ref, h2_ref, route_ref, route_t_ref, cnt_ref, a_scr = rest
    else:
        x1_ref, h2_ref, a_scr = rest
    g1_all = _mod_rows(mp_g1, ms_g1)
    sh_all = _mod_rows(mp_sh, ms_sh)
    sc_all = _mod_rows(mp_sc, ms_sc)
    h2_parts = []
    for rb in range(TM // MIX_ROWS):
        rs = slice(rb * MIX_ROWS, (rb + 1) * MIX_ROWS)
        ss = slice(rb * MIX_ROWS // DEC_SEQ, (rb + 1) * MIX_ROWS // DEC_SEQ)
        for c in range(MIX_ROWS // GM_CHUNK):
            rows = slice(rs.start + c * GM_CHUNK, rs.start + (c + 1) * GM_CHUNK)
            for gi in range(GM_GROUPS):
                cols = slice(gi * GM_GROUP_DIM, (gi + 1) * GM_GROUP_DIM)
                mixed = _dot(wmix_ref[gi], v_ref[rows, cols])
                bias = bmix_ref[gi]
                mixed = mixed + jnp.concatenate([bias] * (GM_GROUP_DIM // LANES), axis=1)
                a_scr[rows, cols] = (u_ref[rows, cols].astype(_F32) * mixed).astype(_BF)
        br_a = _dot(a_scr[rs, :], wa_ref[...])
        o = jnp.where(_is_sample_tile(), os_ref[rs, :], op_ref[rs, :])
        br_b = _dot(o, wb_ref[...])
        merged = sga_ref[rs, :].astype(_F32) * br_a + sgb_ref[rs, :].astype(_F32) * br_b
        mix = _dot(merged.astype(_BF), wo_ref[...])
        g1 = g1_all[ss]
        nm = _rms(mix, npost_ref[...])
        x1 = x[rs, :] + _per_seq(nm, lambda t: t * g1[:, None, :])
        x1_ref[rs, :] = x1
        h2 = _scale_shift_norm(x1, npre_ref[...], sh_all[ss], sc_all[ss])
        h2_ref[rs, :] = h2.astype(_BF)
        h2_parts.append(h2)
    if with_router:
        h2 = jnp.concatenate(h2_parts, axis=0)
        _route(h2, h2_ref[...], wr_hi_ref, wr_lo_ref, br_ref, route_ref, route_t_ref, cnt_ref)


def _mix_call(layer, moe_idx, u, v, sga, sgb, o_p, o_s, x, modp, mods, wmix, bmix,
              wa, wb, wo, npost, npre, wr_hi=None, wr_lo=None, br=None):
    with_router = moe_idx is not None
    split_x = isinstance(x, tuple)
    ty = lambda i: (i >= NT_P).astype(_I32)
    wspec = pl.BlockSpec((None, D_MODEL, D_MODEL), lambda i: (layer, 0, 0))
    in_specs = [_tile_spec(), _tile_spec(), _tile_spec(), _tile_spec()] + _split_specs(HG_WIDTH)
    in_specs += _split_specs() if split_x else [_tile_spec()]
    in_specs += _mod_specs(layer, 2) + _mod_specs(layer, 3) + _mod_specs(layer, 4)
    in_specs += [pl.BlockSpec((None, None, GM_GROUPS, GM_CHUNK, GM_CHUNK), lambda i: (layer, ty(i), 0, 0, 0)),
                 pl.BlockSpec((None, None, GM_GROUPS, GM_CHUNK, LANES), lambda i: (layer, ty(i), 0, 0, 0)),
                 wspec, wspec, wspec, _row_spec(layer), _row_spec(layer)]
    args = [u, v, sga, sgb, o_p, o_s] + (list(x) if split_x else [x])
    args += [modp, mods, modp, mods, modp, mods, wmix, bmix, wa, wb, wo, npost, npre]
    out_specs = [_tile_spec(), _tile_spec()]
    out_shape = [jax.ShapeDtypeStruct((N_TOK, D_MODEL), _F32),
                 jax.ShapeDtypeStruct((N_TOK, D_MODEL), _BF)]
    if with_router:
        rspec = pl.BlockSpec((None, D_MODEL, LANES), lambda i: (moe_idx, 0, 0))
        in_specs += [rspec, rspec, pl.BlockSpec((None, 1, LANES), lambda i: (moe_idx, 0, 0))]
        args += [wr_hi, wr_lo, br]
        out_specs += [_tile_spec(LANES),
                      pl.BlockSpec((None, SUBLANES, TM), lambda i: (i, 0, 0)),
                      pl.BlockSpec((None, SUBLANES, LANES), lambda i: (i, 0, 0))]
        out_shape += [jax.ShapeDtypeStruct((N_TOK, LANES), _F32),
                      jax.ShapeDtypeStruct((NT, SUBLANES, TM), _F32),
                      jax.ShapeDtypeStruct((NT, SUBLANES, LANES), _I32)]
    return pl.pallas_call(
        functools.partial(_mix_kernel, with_router, split_x),
        grid=(NT,),
        in_specs=in_specs,
        out_specs=out_specs,
        out_shape=out_shape,
        scratch_shapes=[pltpu.VMEM((TM, GM_WIDTH), _BF)],
        compiler_params=_cparams(1),
        name="mix",
    )(*args)


def _swiglu(h_ref, wg_ref, wu_ref, wd_ref):
    outs = []
    for rb in range(h_ref.shape[0] // FFN_ROWS):
        h = h_ref[rb * FFN_ROWS:(rb + 1) * FFN_ROWS, :]
        gate = _dot(h, wg_ref[...])
        up = _dot(h, wu_ref[...])
        act = (_silu(gate) * up).astype(_BF)
        outs.append(_dot(act, wd_ref[...]))
    return jnp.concatenate(outs, axis=0)


def _layer_out(last, x1_ref, f, mp_g2, ms_g2, npost_ref, tail_refs):
    g2 = _mod_rows(mp_g2, ms_g2)
    nf = _rms(f, npost_ref[...])
    x2 = x1_ref[...] + _per_seq(nf, lambda t: t * g2[:, None, :])
    if last:
        yp_ref, ys_ref = tail_refs

        @pl.when(_is_sample_tile())
        def _():
            ys_ref[...] = x2

        @pl.when(jnp.logical_not(_is_sample_tile()))
        def _():
            yp_ref[...] = x2
    else:
        mp_sh, ms_sh, mp_sc, ms_sc, nw_ref, x_ref, h_ref = tail_refs
        x_ref[...] = x2
        h_ref[...] = _modulated_norm(x2, nw_ref, mp_sh, ms_sh, mp_sc, ms_sc).astype(_BF)


def _layer_out_specs(layer, last, modp, mods, npre_mix):
    if last:
        return [], [], _split_specs(), [jax.ShapeDtypeStruct((N_PROMPT, D_MODEL), _F32),
                                        jax.ShapeDtypeStruct((N_SAMPLE, D_MODEL), _F32)]
    in_specs = _mod_specs(layer + 1, 0) + _mod_specs(layer + 1, 1) + [_row_spec(layer + 1)]
    return (in_specs, [modp, mods, modp, mods, npre_mix], [_tile_spec(), _tile_spec()],
            [jax.ShapeDtypeStruct((N_TOK, D_MODEL), _F32), jax.ShapeDtypeStruct((N_TOK, D_MODEL), _BF)])


def _ffn_kernel(last, h_ref, x1_ref, mp_g2, ms_g2, wg_ref, wu_ref, wd_ref, npost_ref, *tail_refs):
    f = _swiglu(h_ref, wg_ref, wu_ref, wd_ref)
    _layer_out(last, x1_ref, f, mp_g2, ms_g2, npost_ref, tail_refs)


def _ffn_call(layer, j, h2, x1, modp, mods, wg, wu, wd, npost, npre_mix, rider=None):
    last = layer == DEPTH - 1
    extra_specs, extra_args, out_specs, out_shape = _layer_out_specs(layer, last, modp, mods, npre_mix)
    w_in = pl.BlockSpec((None, D_MODEL, D_FF), lambda i: (j, 0, 0), pipeline_mode=pl.Buffered(1))
    w_out = pl.BlockSpec((None, D_FF, D_MODEL), lambda i: (j, 0, 0), pipeline_mode=pl.Buffered(1))
    in_specs = [_tile_spec(), _tile_spec()] + _mod_specs(layer, 5) + [
        w_in, w_in, w_out, _row_spec(layer)] + extra_specs
    args = [h2, x1, modp, mods, wg, wu, wd, npost, *extra_args]
    body = _add_rider(rider, functools.partial(_ffn_kernel, last), len(args),
                      in_specs, args, out_specs, out_shape)
    return pl.pallas_call(
        body,
        grid=(NT,),
        in_specs=in_specs,
        out_specs=out_specs,
        out_shape=out_shape,
        compiler_params=_cparams(1),
        name="ffn",
    )(*args)


def _route_tables(cnt):
    cnt = cnt[:, 0, :N_EXPERTS]
    n = (cnt + (RUN_ALIGN - 1)) // RUN_ALIGN * RUN_ALIGN
    s = jnp.cumsum(n, axis=1) - n
    rows_e = jnp.sum(n, axis=0)
    region = (rows_e + (TG - 1)) // TG * TG
    region_end = jnp.cumsum(region)
    off = region_end - region
    p = off[None, :] + jnp.cumsum(n, axis=0) - n
    n_tiles = region_end[-1] // TG
    tile_row0 = jnp.minimum(jnp.arange(R_TILES, dtype=_I32), n_tiles - 1) * TG
    tile_expert = jnp.sum((tile_row0[:, None] >= region_end[None, :]).astype(_I32), axis=1)
    flat = lambda a: a.reshape(-1).astype(_I32)
    return dict(p=flat(p), s=flat(s), n=flat(n), tail_p=flat(off + rows_e), tail_n=flat(region - rows_e),
                tile_expert=flat(tile_expert), n_tiles=flat(n_tiles))


def _for_each_run_piece(n, src0, dst0, fn):
    for size in RUN_SIZES:
        done = n & (-2 * size)

        @pl.when((n & size) != 0)
        def _():
            fn(pl.multiple_of(src0 + done, RUN_ALIGN), pl.multiple_of(dst0 + done, RUN_ALIGN), size)


def _dispatch_kernel(p_tab, s_tab, n_tab, tail_p, tail_n, nt_ref, h_ref, rt_ref, xs_ref,
                     comp_scr, zero_scr, sem):
    t = pl.program_id(0)
    pos1 = rt_ref[0:1, :].astype(_I32)
    pos2 = rt_ref[1:2, :].astype(_I32)
    r = lax.broadcasted_iota(_I32, (CB, TM), 0)
    perm = jnp.where(jnp.logical_or(r == pos1, r == pos2), 1.0, 0.0).astype(_BF)
    slot = jnp.bitwise_and(t, 1)
    comp_scr[slot] = _dot(perm, h_ref[...]).astype(_BF)

    def runs(tile, buf, go):
        for e in range(N_EXPERTS):
            idx = tile * N_EXPERTS + e

            def piece(src, dst, size):
                go(pltpu.make_async_copy(comp_scr.at[buf, pl.ds(src, size)], xs_ref.at[pl.ds(dst, size)],
                                         sem.at[buf]))

            _for_each_run_piece(n_tab[idx], s_tab[idx], p_tab[idx], piece)

    runs(t, slot, lambda cp: cp.start())

    @pl.when(t > 0)
    def _():
        runs(t - 1, 1 - slot, lambda cp: cp.wait())

    @pl.when(t == NT - 1)
    def _():
        runs(t, slot, lambda cp: cp.wait())
        zero_scr[...] = jnp.zeros_like(zero_scr)
        zsem = sem.at[0]

        def tails(go):
            for e in range(N_EXPERTS):
                def piece(src, dst, size):
                    go(pltpu.make_async_copy(zero_scr.at[pl.ds(0, size)], xs_ref.at[pl.ds(dst, size)], zsem))

                _for_each_run_piece(tail_n[e], 0, tail_p[e], piece)

        tails(lambda cp: cp.start())
        tails(lambda cp: cp.wait())

        def spare_tile(go):
            def body(r, carry):
                dst = pl.multiple_of(r * TG, TG)
                go(pltpu.make_async_copy(zero_scr, xs_ref.at[pl.ds(dst, TG)], zsem))
                return carry
            lax.fori_loop(nt_ref[0], R_TILES, body, 0)

        spare_tile(lambda cp: cp.start())
        spare_tile(lambda cp: cp.wait())


def _dispatch_call(tabs, h2, route_t):
    grid_spec = pltpu.PrefetchScalarGridSpec(
        num_scalar_prefetch=6,
        grid=(NT,),
        in_specs=[_tile_spec(), pl.BlockSpec((None, SUBLANES, TM), lambda i, *_: (i, 0, 0))],
        out_specs=pl.BlockSpec(memory_space=pl.ANY),
        scratch_shapes=[pltpu.VMEM((2, CB, D_MODEL), _BF), pltpu.VMEM((TG, D_MODEL), _BF),
                        pltpu.SemaphoreType.DMA((2,))])
    return pl.pallas_call(
        _dispatch_kernel,
        grid_spec=grid_spec,
        out_shape=jax.ShapeDtypeStruct((N_ROWS, D_MODEL), _BF),
        compiler_params=_cparams(1),
        name="dispatch",
    )(tabs["p"], tabs["s"], tabs["n"], tabs["tail_p"], tabs["tail_n"], tabs["n_tiles"], h2, route_t)


def _expert_kernel(te_ref, nt_ref, x_ref, wg_ref, wu_ref, wd_ref, y_ref):
    r = pl.program_id(0)

    @pl.when(r < nt_ref[0])
    def _():
        y_ref[...] = _swiglu(x_ref, wg_ref, wu_ref, wd_ref).astype(_BF)

    @pl.when(r >= nt_ref[0])
    def _():
        y_ref[...] = jnp.zeros_like(y_ref)


def _expert_call(tabs, xs, wg, wu, wd):
    wg, wu = wg.reshape(N_EXPERTS, D_MODEL, D_FF), wu.reshape(N_EXPERTS, D_MODEL, D_FF)
    wd = wd.reshape(N_EXPERTS, D_FF, D_MODEL)
    w_in = pl.BlockSpec((None, D_MODEL, D_FF), lambda r, te, nt: (te[r], 0, 0))
    w_out = pl.BlockSpec((None, D_FF, D_MODEL), lambda r, te, nt: (te[r], 0, 0))
    grid_spec = pltpu.PrefetchScalarGridSpec(
        num_scalar_prefetch=2,
        grid=(R_TILES,),
        in_specs=[pl.BlockSpec((TG, D_MODEL), lambda r, te, nt: (jnp.minimum(r, nt[0] - 1), 0)),
                  w_in, w_in, w_out],
        out_specs=pl.BlockSpec((TG, D_MODEL), lambda r, te, nt: (r, 0)))
    return pl.pallas_call(
        _expert_kernel,
        grid_spec=grid_spec,
        out_shape=jax.ShapeDtypeStruct((N_ROWS, D_MODEL), _BF),
        compiler_params=_cparams(1),
        name="experts",
    )(tabs["tile_expert"], tabs["n_tiles"], xs, wg, wu, wd)


def _combine_kernel(last, p_tab, s_tab, n_tab, route_ref, x1_ref, mp_g2, ms_g2, npost_ref, ys_ref, *rest):
    *tail_refs, yc_scr, sem = rest
    t = pl.program_id(0)
    slot = jnp.bitwise_and(t, 1)

    def runs(tile, buf, go):
        for e in range(N_EXPERTS):
            idx = tile * N_EXPERTS + e

            def piece(src, dst, size):
                go(pltpu.make_async_copy(ys_ref.at[pl.ds(dst, size)], yc_scr.at[buf, pl.ds(src, size)],
                                         sem.at[buf]))

            _for_each_run_piece(n_tab[idx], s_tab[idx], p_tab[idx], piece)

    @pl.when(t == 0)
    def _():
        yc_scr[...] = jnp.zeros_like(yc_scr)
        runs(t, slot, lambda cp: cp.start())

    @pl.when(t + 1 < NT)
    def _():
        runs(t + 1, 1 - slot, lambda cp: cp.start())

    runs(t, slot, lambda cp: cp.wait())
    route = route_ref[...]
    pos1 = route[:, 0:1].astype(_I32)
    pos2 = route[:, 1:2].astype(_I32)
    w1 = route[:, 2:3]
    w2 = route[:, 3:4]
    sorted_row = lax.broadcasted_iota(_I32, (TM, CB), 1)
    yc = yc_scr[slot]
    pick1 = jnp.where(sorted_row == pos1, 1.0, 0.0).astype(_BF)
    pick2 = jnp.where(sorted_row == pos2, 1.0, 0.0).astype(_BF)
    f = w1 * _dot(pick1, yc) + w2 * _dot(pick2, yc)
    _layer_out(last, x1_ref, f, mp_g2, ms_g2, npost_ref, tail_refs)


def _combine_call(layer, tabs, route, x1, modp, mods, npost, ys, npre_mix):
    last = layer == DEPTH - 1
    extra_specs, extra_args, out_specs, out_shape = _layer_out_specs(layer, last, modp, mods, npre_mix)
    grid_spec = pltpu.PrefetchScalarGridSpec(
        num_scalar_prefetch=3,
        grid=(NT,),
        in_specs=[_tile_spec(LANES), _tile_spec()] + _mod_specs(layer, 5) + [
            _row_spec(layer), pl.BlockSpec(memory_space=pl.ANY)] + extra_specs,
        out_specs=out_specs,
        scratch_shapes=[pltpu.VMEM((2, CB, D_MODEL), _BF), pltpu.SemaphoreType.DMA((2,))])
    return pl.pallas_call(
        functools.partial(_combine_kernel, last),
        grid_spec=grid_spec,
        out_shape=out_shape,
        compiler_params=_cparams(1),
        name="combine",
    )(tabs["p"], tabs["s"], tabs["n"], route, x1, modp, mods, npost, ys, *extra_args)


def _spatial_tables(gm_ws, gm_bs):
    mask_p = jnp.tril(jnp.ones((GM_CHUNK, GM_CHUNK), bool))
    w_p = jnp.where(mask_p, gm_ws, 0.0)
    mask_s = jnp.tril(jnp.ones((DEC_SEQ, DEC_SEQ), bool))
    w_small = jnp.where(mask_s, gm_ws[:, :, :DEC_SEQ, :DEC_SEQ], 0.0)
    eye = jnp.eye(GM_CHUNK // DEC_SEQ, dtype=gm_ws.dtype)
    w_s = jnp.einsum("ab,lgts->lgatbs", eye, w_small).reshape(gm_ws.shape)
    b_p = gm_bs
    b_s = jnp.tile(gm_bs[:, :, :DEC_SEQ], (1, 1, GM_CHUNK // DEC_SEQ))
    wmix = jnp.stack([w_p, w_s], axis=1).astype(_BF)
    bmix = jnp.stack([b_p, b_s], axis=1)[..., None]
    bmix = jnp.broadcast_to(bmix, bmix.shape[:-1] + (LANES,)).astype(_F32)
    return wmix, bmix


def kernel(x_prompt, x_sample, c_prompt, c_sample, state_hgrn, w_in, gm_ln_w, gm_ln_b, gm_ws, gm_bs,
           hg_lb_raw, hg_norm_w, w_branch_a, w_branch_b, w_out, w_ada, b_ada,
           norm_pre_mix, norm_post_mix, norm_pre_ffn, norm_post_ffn,
           w_ffn_gate, w_ffn_up, w_ffn_down, w_router, b_router, w_exp_gate, w_exp_up, w_exp_down):
    x = (x_prompt.reshape(N_PROMPT, D_MODEL), x_sample.reshape(N_SAMPLE, D_MODEL))
    c_all = jnp.concatenate([c_prompt, c_sample], axis=0)
    mod = _ada_call(c_all, w_ada, b_ada)
    modp = mod[:, :BATCH].reshape(DEPTH, BATCH, 1, 6 * D_MODEL)
    mods = mod[:, BATCH:]
    loglb = _lb_call(hg_lb_raw)
    aux = jnp.zeros((DEPTH, SUBLANES, D_MODEL), _F32)
    aux = aux.at[:, 0].set(gm_ln_w).at[:, 1].set(gm_ln_b).at[:, 2].set(loglb)
    wmix, bmix = _spatial_tables(gm_ws, gm_bs)
    row3 = lambda a: a.reshape(a.shape[0], 1, a.shape[1])
    w_in_bf = w_in.astype(_BF)
    wa_bf, wb_bf, wo_bf = w_branch_a.astype(_BF), w_branch_b.astype(_BF), w_out.astype(_BF)
    wfg, wfu, wfd = w_ffn_gate.astype(_BF), w_ffn_up.astype(_BF), w_ffn_down.astype(_BF)
    wr_pad = jnp.pad(w_router, ((0, 0), (0, 0), (0, LANES - N_EXPERTS)))
    wr_hi = wr_pad.astype(_BF)
    wr_lo = (wr_pad - wr_hi.astype(_F32)).astype(_BF)
    br_pad = jnp.pad(b_router, ((0, 0), (0, LANES - N_EXPERTS))).reshape(-1, 1, LANES)
    npre_mix, npost_mix = row3(norm_pre_mix), row3(norm_post_mix)
    npre_ffn, npost_ffn = row3(norm_pre_ffn), row3(norm_post_ffn)
    hg_nw = row3(hg_norm_w)

    hp, vs = [], []
    sample_states = jnp.zeros(state_hgrn.shape, _F32)
    h = _prenorm_call(0, x[0], x[1], modp, mods, npre_mix)
    for l in range(DEPTH):
        j = l // 2
        dense = l % 2 == 0
        ride = dense and l + 1 < DEPTH
        rider = (lambda w: (w, (l + 1) // 2)) if ride else (lambda w: None)
        act0 = list(_inproj_call(l, ((0, "gelu"), (1, "gelu_ln"), (2, "silu"), (4, "id")), h, w_in_bf, aux,
                                 INPROJ_ROWS_A, rider(w_exp_gate)))
        act4 = list(_inproj_call(l, ((3, "logf"), (5, "silu"), (6, "sigmoid"), (7, "sigmoid")), h, w_in_bf, aux,
                                 INPROJ_ROWS_B, rider(w_exp_up)))
        if ride:
            expert_w = [act0.pop(), act4.pop()]
        u, v, v32, q, iv = act0
        g, sg, sga, sgb = act4
        o_p, s_p = _hgrn_prompt_call(l, q, g, iv, sg, hg_nw)
        o_s, sample_states = _hgrn_sample_call(l, q, g, iv, sg, hg_nw, state_hgrn, sample_states)
        if dense:
            x1, h2 = _mix_call(l, None, u, v, sga, sgb, o_p, o_s, x, modp, mods, wmix, bmix,
                               wa_bf, wb_bf, wo_bf, npost_mix, npre_ffn)
            out = list(_ffn_call(l, j, h2, x1, modp, mods, wfg, wfu, wfd, npost_ffn, npre_mix,
                                 rider(w_exp_down)))
            if ride:
                expert_w.append(out.pop())
        else:
            x1, h2, route, route_t, cnt = _mix_call(l, j, u, v, sga, sgb, o_p, o_s, x, modp, mods, wmix, bmix,
                                                    wa_bf, wb_bf, wo_bf, npost_mix, npre_ffn,
                                                    wr_hi, wr_lo, br_pad)
            tabs = _route_tables(cnt)
            xs = _dispatch_call(tabs, h2, route_t)
            ys = _expert_call(tabs, xs, *expert_w)
            out = _combine_call(l, tabs, route, x1, modp, mods, npost_ffn, ys, npre_mix)
        if l < DEPTH - 1:
            x, h = out
        hp.append(s_p)
        vs.append(v32.reshape(DEC_BATCH, DEC_SEQ, GM_WIDTH))
    y_prompt = out[0].reshape(BATCH, SEQ, D_MODEL)
    y_sample = out[1].reshape(DEC_BATCH, DEC_SEQ, D_MODEL)
    return (y_prompt, y_sample, jnp.stack(hp, axis=0), sample_states, jnp.stack(vs, axis=0))
```

```python
import functools
import math

import jax
import jax.numpy as jnp
from jax import lax
from jax.experimental import pallas as pl
from jax.experimental.pallas import tpu as pltpu

D_MODEL = 1024
BATCH = 8
SEQ = 2048
DEPTH = 4
DEC_BATCH = 128
DEC_SEQ = 8
GM_WIDTH = 1024
GM_GROUPS = 4
GM_GROUP_DIM = GM_WIDTH // GM_GROUPS
GM_CHUNK = 128
HG_HEADS = 8
HG_DK = 128
HG_DV = 128
HG_WIDTH = HG_HEADS * HG_DK
D_FF = 2816
N_EXPERTS = 8
TOP_K = 2
IN_COLS = 2 * GM_WIDTH + 4 * HG_WIDTH + 2 * D_MODEL
EPS = 1e-6

LANES = 128
SUBLANES = 8
BF16_ROWS = 16
TM = 512
N_PROMPT = BATCH * SEQ
N_SAMPLE = DEC_BATCH * DEC_SEQ
N_TOK = N_PROMPT + N_SAMPLE
NT_P = N_PROMPT // TM
NT_S = N_SAMPLE // TM
NT = NT_P + NT_S
TILES_PER_SEQ = SEQ // TM
SEQ_PER_TILE = TM // DEC_SEQ
HG_C = 64
HG_BLK = 256
HG_SB = 16
FFN_ROWS = 256
MIX_ROWS = 256
INPROJ_ROWS_A = 512
INPROJ_ROWS_B = 256
RIDER_BLOCKS = 32
EXP_RANGE = 70.0
VMEM_LIMIT = 56 * 1024 * 1024

TG = 512
RUN_ALIGN = BF16_ROWS
RUN_SIZES = tuple(TM >> k for k in range((TM // RUN_ALIGN).bit_length()))
CB = TOP_K * TM + N_EXPERTS * RUN_ALIGN
_ROWS_BOUND = TOP_K * N_TOK + NT * N_EXPERTS * (RUN_ALIGN - 1) + N_EXPERTS * (TG - RUN_ALIGN)
R_TILES = -(-_ROWS_BOUND // TG)
N_ROWS = R_TILES * TG

_BF = jnp.bfloat16
_F32 = jnp.float32
_I32 = jnp.int32


def _cparams(n_axes):
    return pltpu.CompilerParams(dimension_semantics=("arbitrary",) * n_axes,
                                vmem_limit_bytes=VMEM_LIMIT)


def _dot(a, b):
    return jnp.dot(a, b, preferred_element_type=_F32)


def _sigmoid(x):
    return 0.5 + 0.5 * jnp.tanh(0.5 * x)


def _silu(x):
    hx = 0.5 * x
    return hx + hx * jnp.tanh(hx)


def _gelu(x):
    return 0.5 * x * (1.0 + lax.erf(x * (1.0 / math.sqrt(2.0))))


def _rms(x, w):
    return x * lax.rsqrt(jnp.mean(x * x, axis=-1, keepdims=True) + EPS) * w


def _is_sample_tile():
    return pl.program_id(0) >= NT_P


def _mod_rows(mp_ref, ms_ref):
    mp = jnp.broadcast_to(mp_ref[...], ms_ref.shape)
    return jnp.where(_is_sample_tile(), ms_ref[...], mp)


def _per_seq(x, fn_rows):
    x3 = x.reshape(x.shape[0] // DEC_SEQ, DEC_SEQ, x.shape[-1])
    return fn_rows(x3).reshape(x.shape)


def _mod_specs(layer, col_block):
    mp = pl.BlockSpec((None, None, 1, D_MODEL),
                      lambda i, *_: (layer, jnp.minimum(i // TILES_PER_SEQ, BATCH - 1), 0, col_block))
    ms = pl.BlockSpec((None, SEQ_PER_TILE, D_MODEL),
                      lambda i, *_: (layer, jnp.maximum(i - NT_P, 0), col_block))
    return [mp, ms]


def _tile_spec(width=D_MODEL):
    return pl.BlockSpec((TM, width), lambda i, *_: (i, 0))


def _row_spec(layer, width=D_MODEL):
    return pl.BlockSpec((None, 1, width), lambda i, *_: (layer, 0, 0))


def _ada_kernel(c_ref, w_ref, b_ref, o_ref):
    s = _silu(c_ref[...]).astype(_BF)
    o_ref[...] = _dot(s, w_ref[...].astype(_BF)) + b_ref[...]


def _ada_call(c_all, w_ada, b_ada):
    nb = 4
    wcol = 6 * D_MODEL // nb
    rows = c_all.shape[0]
    return pl.pallas_call(
        _ada_kernel,
        grid=(DEPTH, nb),
        in_specs=[pl.BlockSpec((rows, D_MODEL), lambda l, j: (0, 0)),
                  pl.BlockSpec((None, D_MODEL, wcol), lambda l, j: (l, 0, j)),
                  pl.BlockSpec((None, 1, wcol), lambda l, j: (l, 0, j))],
        out_specs=pl.BlockSpec((None, rows, wcol), lambda l, j: (l, 0, j)),
        out_shape=jax.ShapeDtypeStruct((DEPTH, rows, 6 * D_MODEL), _F32),
        compiler_params=_cparams(2),
        name="ada",
    )(c_all, w_ada, b_ada.reshape(DEPTH, 1, 6 * D_MODEL))


def _lb_kernel(raw_ref, loglb_ref):
    raw = raw_ref[...]
    m = jnp.max(raw, axis=0, keepdims=True)
    e = jnp.exp(raw - m)
    p = e / jnp.sum(e, axis=0, keepdims=True)
    acc = jnp.zeros((1, HG_WIDTH), _F32)
    for l in range(DEPTH):
        if l > 0:
            acc = acc + p[l:l + 1, :]
        loglb_ref[l:l + 1, :] = jnp.log(acc)


def _lb_call(raw):
    return pl.pallas_call(
        _lb_kernel,
        out_shape=jax.ShapeDtypeStruct((DEPTH, HG_WIDTH), _F32),
        name="lower_bounds",
    )(raw)


def _scale_shift_norm(x, nw, sh, sc):
    xn = _rms(x, nw)
    return _per_seq(xn, lambda x3: x3 * (1.0 + sc)[:, None, :] + sh[:, None, :])


def _modulated_norm(x, nw_ref, mp_sh, ms_sh, mp_sc, ms_sc):
    return _scale_shift_norm(x, nw_ref[...], _mod_rows(mp_sh, ms_sh), _mod_rows(mp_sc, ms_sc))


def _prenorm_kernel(xp_ref, xs_ref, mp_sh, ms_sh, mp_sc, ms_sc, nw_ref, h_ref):
    x = jnp.where(_is_sample_tile(), xs_ref[...], xp_ref[...])
    h_ref[...] = _modulated_norm(x, nw_ref, mp_sh, ms_sh, mp_sc, ms_sc).astype(_BF)


def _split_specs(width=D_MODEL):
    return [pl.BlockSpec((TM, width), lambda i, *_: (jnp.minimum(i, NT_P - 1), 0)),
            pl.BlockSpec((TM, width), lambda i, *_: (jnp.maximum(i - NT_P, 0), 0))]


def _prenorm_call(layer, x_p, x_s, modp, mods, norm_w):
    return pl.pallas_call(
        _prenorm_kernel,
        grid=(NT,),
        in_specs=_split_specs() + _mod_specs(layer, 0) + _mod_specs(layer, 1) + [_row_spec(layer)],
        out_specs=_tile_spec(),
        out_shape=jax.ShapeDtypeStruct((N_TOK, D_MODEL), _BF),
        compiler_params=_cparams(1),
        name="prenorm",
    )(x_p, x_s, modp, mods, modp, mods, norm_w)


def _with_cast_rider(body, n_in):
    def wrapped(*refs):
        rider_in, rider_out = refs[n_in], refs[-1]
        rider_out[...] = rider_in[...].astype(_BF)
        body(*refs[:n_in], *refs[n_in + 1:-1])
    return wrapped


def _add_rider(rider, body, n_in, in_specs, args, out_specs, out_shape):
    if rider is None:
        return body
    w, j = rider
    w2 = w.reshape(-1, w.shape[-1])
    rows_per_layer = w2.shape[0] // w.shape[0]
    rows = rows_per_layer // RIDER_BLOCKS
    blk = lambda i: jnp.minimum(i, RIDER_BLOCKS - 1)
    in_specs.append(pl.BlockSpec((rows, w2.shape[1]), lambda i, *_: (j * RIDER_BLOCKS + blk(i), 0)))
    args.append(w2)
    out_specs.append(pl.BlockSpec((rows, w2.shape[1]), lambda i, *_: (blk(i), 0)))
    out_shape.append(jax.ShapeDtypeStruct((rows_per_layer, w2.shape[1]), _BF))
    return _with_cast_rider(body, n_in)


def _inproj_kernel(kinds, row_block, h_ref, aux_ref, *refs):
    w_refs, out_refs = refs[:len(kinds)], refs[len(kinds):]
    for rb in range(TM // row_block):
        rs = slice(rb * row_block, (rb + 1) * row_block)
        h = h_ref[rs, :]
        outs = list(out_refs)
        for j, kind in enumerate(kinds):
            z = _dot(h, w_refs[j][...])
            if kind == "gelu":
                outs.pop(0)[rs, :] = _gelu(z).astype(_BF)
            elif kind == "gelu_ln":
                a = _gelu(z)
                mu = jnp.mean(a, axis=-1, keepdims=True)
                ac = a - mu
                vn = ac * lax.rsqrt(jnp.mean(ac * ac, axis=-1, keepdims=True) + EPS)
                vn = vn * aux_ref[0:1, :] + aux_ref[1:2, :]
                outs.pop(0)[rs, :] = vn.astype(_BF)
                v32_ref = outs.pop(0)

                @pl.when(_is_sample_tile())
                def _():
                    v32_ref[rs, :] = vn
            elif kind == "silu":
                outs.pop(0)[rs, :] = _silu(z).astype(_BF)
            elif kind == "sigmoid":
                outs.pop(0)[rs, :] = _sigmoid(z).astype(_BF)
            elif kind == "id":
                outs.pop(0)[rs, :] = z.astype(_BF)
            elif kind == "logf":
                loglb = aux_ref[2:3, :]
                tail = jnp.log(1.0 + jnp.exp(-jnp.abs(z)))
                a = jnp.minimum(z, 0.0) - tail
                c = loglb - jnp.maximum(z, 0.0) - tail
                outs.pop(0)[rs, :] = jnp.maximum(a, c) + jnp.log(1.0 + jnp.exp(-jnp.abs(z - loglb)))
            else:
                raise ValueError(kind)


def _inproj_call(layer, sections, h, w_in_bf, aux, row_block, rider=None):
    kinds = tuple(kind for _, kind in sections)
    out_specs, out_shape = [], []
    for kind in kinds:
        dt = _F32 if kind == "logf" else _BF
        out_specs.append(_tile_spec())
        out_shape.append(jax.ShapeDtypeStruct((N_TOK, D_MODEL), dt))
        if kind == "gelu_ln":
            out_specs.append(pl.BlockSpec((TM, D_MODEL), lambda i: (jnp.maximum(i - NT_P, 0), 0)))
            out_shape.append(jax.ShapeDtypeStruct((N_SAMPLE, D_MODEL), _F32))
    in_specs = [_tile_spec(), pl.BlockSpec((None, SUBLANES, D_MODEL), lambda i: (layer, 0, 0))]
    in_specs += [pl.BlockSpec((None, D_MODEL, D_MODEL), lambda i, sec=sec: (layer, 0, sec)) for sec, _ in sections]
    args = [h, aux] + [w_in_bf] * len(sections)
    body = _add_rider(rider, functools.partial(_inproj_kernel, kinds, row_block), len(args),
                      in_specs, args, out_specs, out_shape)
    return pl.pallas_call(
        body,
        grid=(NT,),
        in_specs=in_specs,
        out_specs=out_specs,
        out_shape=out_shape,
        compiler_params=_cparams(1),
        name=f"inproj_{sections[0][0]}",
    )(*args)


def _col_bcast(row):
    n = row.shape[-1]
    return jnp.broadcast_to(row, (n, n)).T


def _head_out(o, nw, sg):
    return (_rms(o, nw) * sg.astype(_F32)).astype(_BF)


def _hgrn_exact(ns, rows, q_ref, g_ref, i_ref, sg_ref, nw_ref, o_ref, s_get, s_put):
    n = ns * DEC_SEQ
    shape3 = (ns, DEC_SEQ, HG_WIDTH)
    t_row = jnp.bitwise_and(lax.broadcasted_iota(_I32, (n, HG_WIDTH), 0), DEC_SEQ - 1)
    t3 = lax.broadcasted_iota(_I32, (ns, DEC_SEQ, 1), 1)
    g = g_ref[rows, :]
    b = g
    shift = 1
    while shift < DEC_SEQ:
        b = b + jnp.where(t_row >= shift, pltpu.roll(b, shift, axis=0), 0.0)
        shift *= 2
    kk = 1.0 - jnp.exp(g)
    q = q_ref[rows, :].astype(_F32)
    b3 = b.reshape(shape3)
    bend3 = b3[:, DEC_SEQ - 1:DEC_SEQ, :]
    qe = (q * jnp.exp(b)).astype(_BF)
    kdec = (kk.reshape(shape3) * jnp.exp(bend3 - b3)).reshape(n, HG_WIDTH).astype(_BF)
    e_end = jnp.exp(bend3)
    iv_all = i_ref[rows, :]
    sg_all = sg_ref[rows, :]
    q3, k3, i3 = q.reshape(shape3), kk.reshape(shape3), iv_all.astype(_F32).reshape(shape3)
    for h in range(HG_HEADS):
        hs = slice(h * HG_DK, (h + 1) * HG_DK)
        qh, bh, kh, ih = q3[:, :, hs], b3[:, :, hs], k3[:, :, hs], i3[:, :, hs]
        intra = jnp.zeros((ns, DEC_SEQ, HG_DV), _F32)
        for j in range(DEC_SEQ):
            decay = jnp.exp(jnp.minimum(bh - bh[:, j:j + 1, :], 0.0))
            w = jnp.sum(qh * decay * kh[:, j:j + 1, :], axis=-1, keepdims=True)
            intra = intra + jnp.where(t3 >= j, w, 0.0) * ih[:, j:j + 1, :]
        outs = []
        for si in range(ns):
            rs = slice(si * DEC_SEQ, (si + 1) * DEC_SEQ)
            s_old = s_get(si, h)
            outs.append(intra[si] + _dot(qe[rs, hs], s_old.astype(_BF)))
            upd = lax.dot_general(kdec[rs, hs], iv_all[rs, hs], (((0,), (0,)), ((), ())),
                                  preferred_element_type=_F32)
            s_put(si, h, _col_bcast(e_end[si, :, hs]) * s_old + upd)
        o = outs[0] if ns == 1 else jnp.concatenate(outs, axis=0)
        o_ref[rows, hs] = _head_out(o, nw_ref[...], sg_all[:, hs])


def _rows_of_chunks(x, row_in_chunk):
    parts = []
    for c in range(HG_BLK // HG_C):
        r = c * HG_C + row_in_chunk
        parts.append(jnp.broadcast_to(x[r:r + 1, :], (HG_C, x.shape[-1])))
    return jnp.concatenate(parts, axis=0)


def _hgrn_prompt_kernel(q_ref, g_ref, i_ref, sg_ref, nw_ref, o_ref, s_out_ref, s_scr):
    c = pl.program_id(1)

    @pl.when(c == 0)
    def _():
        s_scr[...] = jnp.zeros_like(s_scr)

    g = g_ref[...]
    row = lax.broadcasted_iota(_I32, (HG_BLK, HG_BLK), 0)
    col = lax.broadcasted_iota(_I32, (HG_BLK, HG_BLK), 1)
    tril = jnp.logical_and(row >= col, row // HG_C == col // HG_C).astype(_BF)
    g_hi = g.astype(_BF)
    r1 = g - g_hi.astype(_F32)
    g_mid = r1.astype(_BF)
    g_lo = (r1 - g_mid.astype(_F32)).astype(_BF)
    b = _dot(tril, g_hi) + _dot(tril, g_mid) + _dot(tril, g_lo)
    ref = _rows_of_chunks(b, HG_C // 2 - 1)
    in_range = jnp.max(jnp.abs(b - ref)) <= EXP_RANGE

    @pl.when(in_range)
    def _():
        kk = 1.0 - jnp.exp(g)
        q = q_ref[...].astype(_F32)
        bend = _rows_of_chunks(b, HG_C - 1)
        qt = (q * jnp.exp(b - ref)).astype(_BF)
        kt = (kk * jnp.exp(ref - b)).astype(_BF)
        qe = (q * jnp.exp(b)).astype(_BF)
        kdec = (kk * jnp.exp(bend - b)).astype(_BF)
        e_end = [jnp.exp(b[(ci + 1) * HG_C - 1:(ci + 1) * HG_C, :]) for ci in range(HG_BLK // HG_C)]
        crow = lax.broadcasted_iota(_I32, (HG_C, HG_C), 0)
        ccol = lax.broadcasted_iota(_I32, (HG_C, HG_C), 1)
        causal = crow >= ccol
        for h in range(HG_HEADS):
            hs = slice(h * HG_DK, (h + 1) * HG_DK)
            s_cur = s_scr[h]
            for ci in range(HG_BLK // HG_C):
                rs = slice(ci * HG_C, (ci + 1) * HG_C)
                iv = i_ref[rs, hs]
                scores = lax.dot_general(qt[rs, hs], kt[rs, hs], (((1,), (1,)), ((), ())),
                                         preferred_element_type=_F32)
                scores = jnp.where(causal, scores, 0.0).astype(_BF)
                o = _dot(scores, iv) + _dot(qe[rs, hs], s_cur.astype(_BF))
                upd = lax.dot_general(kdec[rs, hs], iv, (((0,), (0,)), ((), ())),
                                      preferred_element_type=_F32)
                s_cur = _col_bcast(e_end[ci][:, hs]) * s_cur + upd
                o_ref[rs, hs] = _head_out(o, nw_ref[...], sg_ref[rs, hs])
            s_scr[h] = s_cur

    @pl.when(jnp.logical_not(in_range))
    def _():
        def s_put(si, h, val):
            s_scr[h] = val

        def sub_chunk(k, carry):
            rows = pl.ds(pl.multiple_of(k * DEC_SEQ, DEC_SEQ), DEC_SEQ)
            _hgrn_exact(1, rows, q_ref, g_ref, i_ref, sg_ref, nw_ref, o_ref, lambda si, h: s_scr[h], s_put)
            return carry

        lax.fori_loop(0, HG_BLK // DEC_SEQ, sub_chunk, 0)

    @pl.when(c == pl.num_programs(1) - 1)
    def _():
        s_out_ref[...] = s_scr[...]


def _hgrn_prompt_call(layer, q, g, iv, sg, hg_norm_w):
    nc = SEQ // HG_BLK
    blk = pl.BlockSpec((HG_BLK, HG_WIDTH), lambda b, c: (b * nc + c, 0))
    return pl.pallas_call(
        _hgrn_prompt_kernel,
        grid=(BATCH, nc),
        in_specs=[blk, blk, blk, blk,
                  pl.BlockSpec((None, 1, HG_DV), lambda b, c: (layer, 0, 0))],
        out_specs=[blk,
                   pl.BlockSpec((None, HG_HEADS, HG_DK, HG_DV), lambda b, c: (b, 0, 0, 0))],
        out_shape=[jax.ShapeDtypeStruct((N_PROMPT, HG_WIDTH), _BF),
                   jax.ShapeDtypeStruct((BATCH, HG_HEADS, HG_DK, HG_DV), _F32)],
        scratch_shapes=[pltpu.VMEM((HG_HEADS, HG_DK, HG_DV), _F32)],
        compiler_params=_cparams(2),
        name="hgrn_prompt",
    )(q, g, iv, sg, hg_norm_w)


def _hgrn_sample_kernel(q_ref, g_ref, i_ref, sg_ref, nw_ref, s_in_ref, all_states_ref, o_ref, s_out_ref):
    del all_states_ref

    def s_put(si, h, val):
        s_out_ref[si, h] = val

    _hgrn_exact(HG_SB, slice(None), q_ref, g_ref, i_ref, sg_ref, nw_ref, o_ref, lambda si, h: s_in_ref[si, h], s_put)


def _hgrn_sample_call(layer, q, g, iv, sg, hg_norm_w, state, all_states):
    rows = HG_SB * DEC_SEQ
    off = N_PROMPT // rows
    blk = pl.BlockSpec((rows, HG_WIDTH), lambda j: (off + j, 0))
    s_blk = pl.BlockSpec((None, HG_SB, HG_HEADS, HG_DK, HG_DV), lambda j: (layer, j, 0, 0, 0))
    return pl.pallas_call(
        _hgrn_sample_kernel,
        grid=(DEC_BATCH // HG_SB,),
        in_specs=[blk, blk, blk, blk,
                  pl.BlockSpec((None, 1, HG_DV), lambda j: (layer, 0, 0)),
                  s_blk, pl.BlockSpec(memory_space=pl.ANY)],
        out_specs=[pl.BlockSpec((rows, HG_WIDTH), lambda j: (j, 0)), s_blk],
        out_shape=[jax.ShapeDtypeStruct((N_SAMPLE, HG_WIDTH), _BF),
                   jax.ShapeDtypeStruct(all_states.shape, _F32)],
        input_output_aliases={6: 1},
        compiler_params=_cparams(1),
        name="hgrn_sample",
    )(q, g, iv, sg, hg_norm_w, state, all_states)


def _route(h2, h2_bf, wr_hi_ref, wr_lo_ref, br_ref, route_ref, route_t_ref, cnt_ref):
    h2_lo = (h2 - h2_bf.astype(_F32)).astype(_BF)
    logits = (_dot(h2_bf, wr_hi_ref[...]) + (_dot(h2_lo, wr_hi_ref[...]) + _dot(h2_bf, wr_lo_ref[...]))
              + br_ref[...])
    lane = lax.broadcasted_iota(_I32, logits.shape, 1)
    neg = jnp.float32(-jnp.inf)
    logits = jnp.where(lane < N_EXPERTS, logits, neg)
    m1 = jnp.max(logits, axis=-1, keepdims=True)
    i1 = jnp.min(jnp.where(logits == m1, lane, LANES), axis=-1, keepdims=True)
    rest_l = jnp.where(lane == i1, neg, logits)
    m2 = jnp.max(rest_l, axis=-1, keepdims=True)
    i2 = jnp.min(jnp.where(rest_l == m2, lane, LANES), axis=-1, keepdims=True)
    e2 = jnp.exp(m2 - m1)
    w1 = 1.0 / (1.0 + e2)
    w2 = e2 / (1.0 + e2)
    hot1 = lane == i1
    hot2 = lane == i2
    tot = jnp.where(jnp.logical_or(hot1, hot2), 1.0, 0.0)
    trow = lax.broadcasted_iota(_I32, (TM, TM), 0)
    tcol = lax.broadcasted_iota(_I32, (TM, TM), 1)
    before = _dot((trow > tcol).astype(_BF), tot.astype(_BF))
    cnt = jnp.sum(tot, axis=0, keepdims=True)
    cnt_pad = jnp.ceil(cnt * (1.0 / RUN_ALIGN)) * RUN_ALIGN
    erow = lax.broadcasted_iota(_I32, (LANES, LANES), 0)
    ecol = lax.broadcasted_iota(_I32, (LANES, LANES), 1)
    start = _dot(jnp.broadcast_to(cnt_pad, (SUBLANES, LANES)).astype(_BF),
                 (erow < ecol).astype(_BF))[0:1, :]
    slot = before + start
    pos1 = jnp.sum(jnp.where(hot1, slot, 0.0), axis=-1, keepdims=True)
    pos2 = jnp.sum(jnp.where(hot2, slot, 0.0), axis=-1, keepdims=True)
    route = jnp.where(lane == 0, pos1, jnp.where(lane == 1, pos2,
                      jnp.where(lane == 2, w1, jnp.where(lane == 3, w2, 0.0))))
    route_ref[...] = route
    route_t_ref[...] = route.T[0:SUBLANES, :]
    cnt_ref[...] = jnp.broadcast_to(cnt, (SUBLANES, LANES)).astype(_I32)


def _mix_kernel(with_router, split_x, u_ref, v_ref, sga_ref, sgb_ref, op_ref, os_ref, *rest):
    if split_x:
        xp_ref, xs_ref, *rest = rest
        x = jnp.where(_is_sample_tile(), xs_ref[...], xp_ref[...])
    else:
        x_ref, *rest = rest
        x = x_ref[...]
    (mp_g1, ms_g1, mp_sh, ms_sh, mp_sc, ms_sc,
     wmix_ref, bmix_ref, wa_ref, wb_ref, wo_ref, npost_ref, npre_ref, *rest) = rest
    if with_router:
        wr_hi_ref, wr_lo_ref, br_ref, x1_ref, h2_ref, route_ref, route_t_ref, cnt_ref, a_scr = rest
    else:
        x1_ref, h2_ref, a_scr = rest
    g1_all = _mod_rows(mp_g1, ms_g1)
    sh_all = _mod_rows(mp_sh, ms_sh)
    sc_all = _mod_rows(mp_sc, ms_sc)
    h2_parts = []
    for rb in range(TM // MIX_ROWS):
        rs = slice(rb * MIX_ROWS, (rb + 1) * MIX_ROWS)
        ss = slice(rb * MIX_ROWS // DEC_SEQ, (rb + 1) * MIX_ROWS // DEC_SEQ)
        for c in range(MIX_ROWS // GM_CHUNK):
            rows = slice(rs.start + c * GM_CHUNK, rs.start + (c + 1) * GM_CHUNK)
            for gi in range(GM_GROUPS):
                cols = slice(gi * GM_GROUP_DIM, (gi + 1) * GM_GROUP_DIM)
                mixed = _dot(wmix_ref[gi], v_ref[rows, cols])
                bias = bmix_ref[gi]
                mixed = mixed + jnp.concatenate([bias] * (GM_GROUP_DIM // LANES), axis=1)
                a_scr[rows, cols] = (u_ref[rows, cols].astype(_F32) * mixed).astype(_BF)
        br_a = _dot(a_scr[rs, :], wa_ref[...])
        o = jnp.where(_is_sample_tile(), os_ref[rs, :], op_ref[rs, :])
        br_b = _dot(o, wb_ref[...])
        merged = sga_ref[rs, :].astype(_F32) * br_a + sgb_ref[rs, :].astype(_F32) * br_b
        mix = _dot(merged.astype(_BF), wo_ref[...])
        g1 = g1_all[ss]
        nm = _rms(mix, npost_ref[...])
        x1 = x[rs, :] + _per_seq(nm, lambda t: t * g1[:, None, :])
        x1_ref[rs, :] = x1
        h2 = _scale_shift_norm(x1, npre_ref[...], sh_all[ss], sc_all[ss])
        h2_ref[rs, :] = h2.astype(_BF)
        h2_parts.append(h2)
    if with_router:
        h2 = jnp.concatenate(h2_parts, axis=0)
        _route(h2, h2_ref[...], wr_hi_ref, wr_lo_ref, br_ref, route_ref, route_t_ref, cnt_ref)


def _mix_call(layer, moe_idx, u, v, sga, sgb, o_p, o_s, x, modp, mods, wmix, bmix,
              wa, wb, wo, npost, npre, wr_hi=None, wr_lo=None, br=None):
    with_router = moe_idx is not None
    split_x = isinstance(x, tuple)
    ty = lambda i: (i >= NT_P).astype(_I32)
    wspec = pl.BlockSpec((None, D_MODEL, D_MODEL), lambda i: (layer, 0, 0))
    in_specs = [_tile_spec(), _tile_spec(), _tile_spec(), _tile_spec()] + _split_specs(HG_WIDTH)
    in_specs += _split_specs() if split_x else [_tile_spec()]
    in_specs += _mod_specs(layer, 2) + _mod_specs(layer, 3) + _mod_specs(layer, 4)
    in_specs += [pl.BlockSpec((None, None, GM_GROUPS, GM_CHUNK, GM_CHUNK), lambda i: (layer, ty(i), 0, 0, 0)),
                 pl.BlockSpec((None, None, GM_GROUPS, GM_CHUNK, LANES), lambda i: (layer, ty(i), 0, 0, 0)),
                 wspec, wspec, wspec, _row_spec(layer), _row_spec(layer)]
    args = [u, v, sga, sgb, o_p, o_s] + (list(x) if split_x else [x])
    args += [modp, mods, modp, mods, modp, mods, wmix, bmix, wa, wb, wo, npost, npre]
    out_specs = [_tile_spec(), _tile_spec()]
    out_shape = [jax.ShapeDtypeStruct((N_TOK, D_MODEL), _F32),
                 jax.ShapeDtypeStruct((N_TOK, D_MODEL), _BF)]
    if with_router:
        rspec = pl.BlockSpec((None, D_MODEL, LANES), lambda i: (moe_idx, 0, 0))
        in_specs += [rspec, rspec, pl.BlockSpec((None, 1, LANES), lambda i: (moe_idx, 0, 0))]
        args += [wr_hi, wr_lo, br]
        out_specs += [_tile_spec(LANES),
                      pl.BlockSpec((None, SUBLANES, TM), lambda i: (i, 0, 0)),
                      pl.BlockSpec((None, SUBLANES, LANES), lambda i: (i, 0, 0))]
        out_shape += [jax.ShapeDtypeStruct((N_TOK, LANES), _F32),
                      jax.ShapeDtypeStruct((NT, SUBLANES, TM), _F32),
                      jax.ShapeDtypeStruct((NT, SUBLANES, LANES), _I32)]
    return pl.pallas_call(
        functools.partial(_mix_kernel, with_router, split_x),
        grid=(NT,),
        in_specs=in_specs,
        out_specs=out_specs,
        out_shape=out_shape,
        scratch_shapes=[pltpu.VMEM((TM, GM_WIDTH), _BF)],
        compiler_params=_cparams(1),
        name="mix",
    )(*args)


def _swiglu(h_ref, wg_ref, wu_ref, wd_ref):
    outs = []
    for rb in range(h_ref.shape[0] // FFN_ROWS):
        h = h_ref[rb * FFN_ROWS:(rb + 1) * FFN_ROWS, :]
        gate = _dot(h, wg_ref[...])
        up = _dot(h, wu_ref[...])
        act = (_silu(gate) * up).astype(_BF)
        outs.append(_dot(act, wd_ref[...]))
    return jnp.concatenate(outs, axis=0)


def _layer_out(last, x1_ref, f, mp_g2, ms_g2, npost_ref, tail_refs):
    g2 = _mod_rows(mp_g2, ms_g2)
    nf = _rms(f, npost_ref[...])
    x2 = x1_ref[...] + _per_seq(nf, lambda t: t * g2[:, None, :])
    if last:
        yp_ref, ys_ref = tail_refs

        @pl.when(_is_sample_tile())
        def _():
            ys_ref[...] = x2

        @pl.when(jnp.logical_not(_is_sample_tile()))
        def _():
            yp_ref[...] = x2
    else:
        mp_sh, ms_sh, mp_sc, ms_sc, nw_ref, x_ref, h_ref = tail_refs
        x_ref[...] = x2
        h_ref[...] = _modulated_norm(x2, nw_ref, mp_sh, ms_sh, mp_sc, ms_sc).astype(_BF)


def _layer_out_specs(layer, last, modp, mods, npre_mix):
    if last:
        return [], [], _split_specs(), [jax.ShapeDtypeStruct((N_PROMPT, D_MODEL), _F32),
                                        jax.ShapeDtypeStruct((N_SAMPLE, D_MODEL), _F32)]
    in_specs = _mod_specs(layer + 1, 0) + _mod_specs(layer + 1, 1) + [_row_spec(layer + 1)]
    return (in_specs, [modp, mods, modp, mods, npre_mix], [_tile_spec(), _tile_spec()],
            [jax.ShapeDtypeStruct((N_TOK, D_MODEL), _F32), jax.ShapeDtypeStruct((N_TOK, D_MODEL), _BF)])


def _ffn_kernel(last, h_ref, x1_ref, mp_g2, ms_g2, wg_ref, wu_ref, wd_ref, npost_ref, *tail_refs):
    f = _swiglu(h_ref, wg_ref, wu_ref, wd_ref)
    _layer_out(last, x1_ref, f, mp_g2, ms_g2, npost_ref, tail_refs)


def _ffn_call(layer, j, h2, x1, modp, mods, wg, wu, wd, npost, npre_mix, rider=None):
    last = layer == DEPTH - 1
    extra_specs, extra_args, out_specs, out_shape = _layer_out_specs(layer, last, modp, mods, npre_mix)
    w_in = pl.BlockSpec((None, D_MODEL, D_FF), lambda i: (j, 0, 0), pipeline_mode=pl.Buffered(1))
    w_out = pl.BlockSpec((None, D_FF, D_MODEL), lambda i: (j, 0, 0), pipeline_mode=pl.Buffered(1))
    in_specs = [_tile_spec(), _tile_spec()] + _mod_specs(layer, 5) + [
        w_in, w_in, w_out, _row_spec(layer)] + extra_specs
    args = [h2, x1, modp, mods, wg, wu, wd, npost, *extra_args]
    body = _add_rider(rider, functools.partial(_ffn_kernel, last), len(args),
                      in_specs, args, out_specs, out_shape)
    return pl.pallas_call(
        body,
        grid=(NT,),
        in_specs=in_specs,
        out_specs=out_specs,
        out_shape=out_shape,
        compiler_params=_cparams(1),
        name="ffn",
    )(*args)


def _route_tables(cnt):
    cnt = cnt[:, 0, :N_EXPERTS]
    n = (cnt + (RUN_ALIGN - 1)) // RUN_ALIGN * RUN_ALIGN
    s = jnp.cumsum(n, axis=1) - n
    rows_e = jnp.sum(n, axis=0)
    region = (rows_e + (TG - 1)) // TG * TG
    region_end = jnp.cumsum(region)
    off = region_end - region
    p = off[None, :] + jnp.cumsum(n, axis=0) - n
    n_tiles = region_end[-1] // TG
    tile_row0 = jnp.minimum(jnp.arange(R_TILES, dtype=_I32), n_tiles - 1) * TG
    tile_expert = jnp.sum((tile_row0[:, None] >= region_end[None, :]).astype(_I32), axis=1)
    flat = lambda a: a.reshape(-1).astype(_I32)
    return dict(p=flat(p), s=flat(s), n=flat(n), tail_p=flat(off + rows_e), tail_n=flat(region - rows_e),
                tile_expert=flat(tile_expert), n_tiles=flat(n_tiles))


def _for_each_run_piece(n, src0, dst0, fn):
    for size in RUN_SIZES:
        done = n & (-2 * size)

        @pl.when((n & size) != 0)
        def _():
            fn(pl.multiple_of(src0 + done, RUN_ALIGN), pl.multiple_of(dst0 + done, RUN_ALIGN), size)


def _dispatch_kernel(p_tab, s_tab, n_tab, tail_p, tail_n, nt_ref, h_ref, rt_ref, xs_ref,
                     comp_scr, zero_scr, sem):
    t = pl.program_id(0)
    pos1 = rt_ref[0:1, :].astype(_I32)
    pos2 = rt_ref[1:2, :].astype(_I32)
    r = lax.broadcasted_iota(_I32, (CB, TM), 0)
    perm = jnp.where(jnp.logical_or(r == pos1, r == pos2), 1.0, 0.0).astype(_BF)
    slot = jnp.bitwise_and(t, 1)
    comp_scr[slot] = _dot(perm, h_ref[...]).astype(_BF)

    def runs(tile, buf, go):
        for e in range(N_EXPERTS):
            idx = tile * N_EXPERTS + e

            def piece(src, dst, size):
                go(pltpu.make_async_copy(comp_scr.at[buf, pl.ds(src, size)], xs_ref.at[pl.ds(dst, size)],
                                         sem.at[buf]))

            _for_each_run_piece(n_tab[idx], s_tab[idx], p_tab[idx], piece)

    runs(t, slot, lambda cp: cp.start())

    @pl.when(t > 0)
    def _():
        runs(t - 1, 1 - slot, lambda cp: cp.wait())

    @pl.when(t == NT - 1)
    def _():
        runs(t, slot, lambda cp: cp.wait())
        zero_scr[...] = jnp.zeros_like(zero_scr)
        zsem = sem.at[0]

        def tails(go):
            for e in range(N_EXPERTS):
                def piece(src, dst, size):
                    go(pltpu.make_async_copy(zero_scr.at[pl.ds(0, size)], xs_ref.at[pl.ds(dst, size)], zsem))

                _for_each_run_piece(tail_n[e], 0, tail_p[e], piece)

        tails(lambda cp: cp.start())
        tails(lambda cp: cp.wait())

        def spare_tile(go):
            def body(r, carry):
                dst = pl.multiple_of(r * TG, TG)
                go(pltpu.make_async_copy(zero_scr, xs_ref.at[pl.ds(dst, TG)], zsem))
                return carry
            lax.fori_loop(nt_ref[0], R_TILES, body, 0)

        spare_tile(lambda cp: cp.start())
        spare_tile(lambda cp: cp.wait())


def _dispatch_call(tabs, h2, route_t):
    grid_spec = pltpu.PrefetchScalarGridSpec(
        num_scalar_prefetch=6,
        grid=(NT,),
        in_specs=[_tile_spec(), pl.BlockSpec((None, SUBLANES, TM), lambda i, *_: (i, 0, 0))],
        out_specs=pl.BlockSpec(memory_space=pl.ANY),
        scratch_shapes=[pltpu.VMEM((2, CB, D_MODEL), _BF), pltpu.VMEM((TG, D_MODEL), _BF),
                        pltpu.SemaphoreType.DMA((2,))])
    return pl.pallas_call(
        _dispatch_kernel,
        grid_spec=grid_spec,
        out_shape=jax.ShapeDtypeStruct((N_ROWS, D_MODEL), _BF),
        compiler_params=_cparams(1),
        name="dispatch",
    )(tabs["p"], tabs["s"], tabs["n"], tabs["tail_p"], tabs["tail_n"], tabs["n_tiles"], h2, route_t)


def _expert_kernel(te_ref, nt_ref, x_ref, wg_ref, wu_ref, wd_ref, y_ref):
    r = pl.program_id(0)

    @pl.when(r < nt_ref[0])
    def _():
        y_ref[...] = _swiglu(x_ref, wg_ref, wu_ref, wd_ref).astype(_BF)

    @pl.when(r >= nt_ref[0])
    def _():
        y_ref[...] = jnp.zeros_like(y_ref)


def _expert_call(tabs, xs, wg, wu, wd):
    wg, wu = wg.reshape(N_EXPERTS, D_MODEL, D_FF), wu.reshape(N_EXPERTS, D_MODEL, D_FF)
    wd = wd.reshape(N_EXPERTS, D_FF, D_MODEL)
    w_in = pl.BlockSpec((None, D_MODEL, D_FF), lambda r, te, nt: (te[r], 0, 0))
    w_out = pl.BlockSpec((None, D_FF, D_MODEL), lambda r, te, nt: (te[r], 0, 0))
    grid_spec = pltpu.PrefetchScalarGridSpec(
        num_scalar_prefetch=2,
        grid=(R_TILES,),
        in_specs=[pl.BlockSpec((TG, D_MODEL), lambda r, te, nt: (jnp.minimum(r, nt[0] - 1), 0)),
                  w_in, w_in, w_out],
        out_specs=pl.BlockSpec((TG, D_MODEL), lambda r, te, nt: (r, 0)))
    return pl.pallas_call(
        _expert_kernel,
        grid_spec=grid_spec,
        out_shape=jax.ShapeDtypeStruct((N_ROWS, D_MODEL), _BF),
        compiler_params=_cparams(1),
        name="experts",
    )(tabs["tile_expert"], tabs["n_tiles"], xs, wg, wu, wd)


def _combine_kernel(last, p_tab, s_tab, n_tab, route_ref, x1_ref, mp_g2, ms_g2, npost_ref, ys_ref, *rest):
    *tail_refs, yc_scr, sem = rest
    t = pl.program_id(0)
    slot = jnp.bitwise_and(t, 1)

    def runs(tile, buf, go):
        for e in range(N_EXPERTS):
            idx = tile * N_EXPERTS + e

            def piece(src, dst, size):
                go(pltpu.make_async_copy(ys_ref.at[pl.ds(dst, size)], yc_scr.at[buf, pl.ds(src, size)],
                                         sem.at[buf]))

            _for_each_run_piece(n_tab[idx], s_tab[idx], p_tab[idx], piece)

    @pl.when(t == 0)
    def _():
        yc_scr[...] = jnp.zeros_like(yc_scr)
        runs(t, slot, lambda cp: cp.start())

    @pl.when(t + 1 < NT)
    def _():
        runs(t + 1, 1 - slot, lambda cp: cp.start())

    runs(t, slot, lambda cp: cp.wait())
    route = route_ref[...]
    pos1 = route[:, 0:1].astype(_I32)
    pos2 = route[:, 1:2].astype(_I32)
    w1 = route[:, 2:3]
    w2 = route[:, 3:4]
    sorted_row = lax.broadcasted_iota(_I32, (TM, CB), 1)
    yc = yc_scr[slot]
    pick1 = jnp.where(sorted_row == pos1, 1.0, 0.0).astype(_BF)
    pick2 = jnp.where(sorted_row == pos2, 1.0, 0.0).astype(_BF)
    f = w1 * _dot(pick1, yc) + w2 * _dot(pick2, yc)
    _layer_out(last, x1_ref, f, mp_g2, ms_g2, npost_ref, tail_refs)


def _combine_call(layer, tabs, route, x1, modp, mods, npost, ys, npre_mix):
    last = layer == DEPTH - 1
    extra_specs, extra_args, out_specs, out_shape = _layer_out_specs(layer, last, modp, mods, npre_mix)
    grid_spec = pltpu.PrefetchScalarGridSpec(
        num_scalar_prefetch=3,
        grid=(NT,),
        in_specs=[_tile_spec(LANES), _tile_spec()] + _mod_specs(layer, 5) + [
            _row_spec(layer), pl.BlockSpec(memory_space=pl.ANY)] + extra_specs,
        out_specs=out_specs,
        scratch_shapes=[pltpu.VMEM((2, CB, D_MODEL), _BF), pltpu.SemaphoreType.DMA((2,))])
    return pl.pallas_call(
        functools.partial(_combine_kernel, last),
        grid_spec=grid_spec,
        out_shape=out_shape,
        compiler_params=_cparams(1),
        name="combine",
    )(tabs["p"], tabs["s"], tabs["n"], route, x1, modp, mods, npost, ys, *extra_args)


def _spatial_tables(gm_ws, gm_bs):
    mask_p = jnp.tril(jnp.ones((GM_CHUNK, GM_CHUNK), bool))
    w_p = jnp.where(mask_p, gm_ws, 0.0)
    mask_s = jnp.tril(jnp.ones((DEC_SEQ, DEC_SEQ), bool))
    w_small = jnp.where(mask_s, gm_ws[:, :, :DEC_SEQ, :DEC_SEQ], 0.0)
    eye = jnp.eye(GM_CHUNK // DEC_SEQ, dtype=gm_ws.dtype)
    w_s = jnp.einsum("ab,lgts->lgatbs", eye, w_small).reshape(gm_ws.shape)
    b_p = gm_bs
    b_s = jnp.tile(gm_bs[:, :, :DEC_SEQ], (1, 1, GM_CHUNK // DEC_SEQ))
    wmix = jnp.stack([w_p, w_s], axis=1).astype(_BF)
    bmix = jnp.stack([b_p, b_s], axis=1)[..., None]
    bmix = jnp.broadcast_to(bmix, bmix.shape[:-1] + (LANES,)).astype(_F32)
    return wmix, bmix


def kernel(x_prompt, x_sample, c_prompt, c_sample, state_hgrn, w_in, gm_ln_w, gm_ln_b, gm_ws, gm_bs,
           hg_lb_raw, hg_norm_w, w_branch_a, w_branch_b, w_out, w_ada, b_ada,
           norm_pre_mix, norm_post_mix, norm_pre_ffn, norm_post_ffn,
           w_ffn_gate, w_ffn_up, w_ffn_down, w_router, b_router, w_exp_gate, w_exp_up, w_exp_down):
    x = (x_prompt.reshape(N_PROMPT, D_MODEL), x_sample.reshape(N_SAMPLE, D_MODEL))
    c_all = jnp.concatenate([c_prompt, c_sample], axis=0)
    mod = _ada_call(c_all, w_ada, b_ada)
    modp = mod[:, :BATCH].reshape(DEPTH, BATCH, 1, 6 * D_MODEL)
    mods = mod[:, BATCH:]
    loglb = _lb_call(hg_lb_raw)
    aux = jnp.zeros((DEPTH, SUBLANES, D_MODEL), _F32)
    aux = aux.at[:, 0].set(gm_ln_w).at[:, 1].set(gm_ln_b).at[:, 2].set(loglb)
    wmix, bmix = _spatial_tables(gm_ws, gm_bs)
    row3 = lambda a: a.reshape(a.shape[0], 1, a.shape[1])
    w_in_bf = w_in.astype(_BF)
    wa_bf, wb_bf, wo_bf = w_branch_a.astype(_BF), w_branch_b.astype(_BF), w_out.astype(_BF)
    wfg, wfu, wfd = w_ffn_gate.astype(_BF), w_ffn_up.astype(_BF), w_ffn_down.astype(_BF)
    wr_pad = jnp.pad(w_router, ((0, 0), (0, 0), (0, LANES - N_EXPERTS)))
    wr_hi = wr_pad.astype(_BF)
    wr_lo = (wr_pad - wr_hi.astype(_F32)).astype(_BF)
    br_pad = jnp.pad(b_router, ((0, 0), (0, LANES - N_EXPERTS))).reshape(-1, 1, LANES)
    npre_mix, npost_mix = row3(norm_pre_mix), row3(norm_post_mix)
    npre_ffn, npost_ffn = row3(norm_pre_ffn), row3(norm_post_ffn)
    hg_nw = row3(hg_norm_w)

    hp, vs = [], []
    sample_states = jnp.zeros(state_hgrn.shape, _F32)
    h = _prenorm_call(0, x[0], x[1], modp, mods, npre_mix)
    for l in range(DEPTH):
        j = l // 2
        dense = l % 2 == 0
        ride = dense and l + 1 < DEPTH
        rider = (lambda w: (w, (l + 1) // 2)) if ride else (lambda w: None)
        act0 = list(_inproj_call(l, ((0, "gelu"), (1, "gelu_ln"), (2, "silu"), (3, "logf")), h, w_in_bf, aux,
                                 INPROJ_ROWS_A, rider(w_exp_gate)))
        act4 = list(_inproj_call(l, ((4, "id"), (5, "silu"), (6, "sigmoid"), (7, "sigmoid")), h, w_in_bf, aux,
                                 INPROJ_ROWS_B, rider(w_exp_up)))
        if ride:
            expert_w = [act0.pop(), act4.pop()]
        u, v, v32, q, g = act0
        iv, sg, sga, sgb = act4
        o_p, s_p = _hgrn_prompt_call(l, q, g, iv, sg, hg_nw)
        o_s, sample_states = _hgrn_sample_call(l, q, g, iv, sg, hg_nw, state_hgrn, sample_states)
        if dense:
            x1, h2 = _mix_call(l, None, u, v, sga, sgb, o_p, o_s, x, modp, mods, wmix, bmix,
                               wa_bf, wb_bf, wo_bf, npost_mix, npre_ffn)
            out = list(_ffn_call(l, j, h2, x1, modp, mods, wfg, wfu, wfd, npost_ffn, npre_mix,
                                 rider(w_exp_down)))
            if ride:
                expert_w.append(out.pop())
        else:
            x1, h2, route, route_t, cnt = _mix_call(l, j, u, v, sga, sgb, o_p, o_s, x, modp, mods, wmix, bmix,
                                                    wa_bf, wb_bf, wo_bf, npost_mix, npre_ffn,
                                                    wr_hi, wr_lo, br_pad)
            tabs = _route_tables(cnt)
            xs = _dispatch_call(tabs, h2, route_t)
            ys = _expert_call(tabs, xs, *expert_w)
            out = _combine_call(l, tabs, route, x1, modp, mods, npost_ffn, ys, npre_mix)
        if l < DEPTH - 1:
            x, h = out
        hp.append(s_p)
        vs.append(v32.reshape(DEC_BATCH, DEC_SEQ, GM_WIDTH))
    y_prompt = out[0].reshape(BATCH, SEQ, D_MODEL)
    y_sample = out[1].reshape(DEC_BATCH, DEC_SEQ, D_MODEL)
    return (y_prompt, y_sample, jnp.stack(hp, axis=0), sample_states, jnp.stack(vs, axis=0))
```
